```python
import math
import jax, jax.numpy as jnp
from jax import lax
import numpy as np

D_MODEL = 1024
BATCH = 8
SEQ = 2048
DEPTH = 2
DEC_BATCH = 32
DEC_SEQ = 4
PAST_LEN = 16384
PAGE_SIZE = 128

N_A_LAYERS = DEPTH // 2
N_B_LAYERS = DEPTH - N_A_LAYERS

ROPE_THETA = 500000.0
EPS = 1e-6
NEG = -1e30

MLA_HEADS = 16
MLA_NOPE = 64
MLA_ROPE = 32
MLA_V = 64
MLA_Q_LORA = 384
MLA_KV_LORA = 256
MLA_LAT = MLA_KV_LORA + MLA_ROPE
Q_BLOCK = 128

DIL_WINDOWS = (128, 512, 2048)
DIL_RATES = (1, 4, 16)
N_GROUPS = 3
DIL_HEADS = 8
DIL_HEAD_DIM = 128
DIL_ROT = DIL_HEAD_DIM // 4

PEER_HEADS = 8
PEER_NKEYS = 128
PEER_EXPERTS = PEER_NKEYS * PEER_NKEYS
PEER_TOPK = 16
PEER_DKEY = 128
PEER_BLOCK = 128

kernel_name = 'hybrid_mla_dilated_peer_step'


def rms(x):
    xf = x.astype(jnp.float32)
    return (xf * lax.rsqrt(jnp.mean(xf * xf, axis=-1, keepdims=True) + EPS)).astype(x.dtype)


def rope(x, pos):
    half = x.shape[-1] // 2
    inv = ROPE_THETA ** (-jnp.arange(half, dtype=jnp.float32) / half)
    ang = pos.astype(jnp.float32)[:, None] * inv
    ang = ang.reshape(ang.shape[:1] + (1,) * (x.ndim - 3) + (half,))
    cos, sin = jnp.cos(ang), jnp.sin(ang)
    xf = x.astype(jnp.float32)
    x1, x2 = xf[..., :half], xf[..., half:]
    return jnp.concatenate([x1 * cos - x2 * sin, x2 * cos + x1 * sin], axis=-1).astype(x.dtype)


def partial_rope(x, pos):
    return jnp.concatenate([rope(x[..., :DIL_ROT], pos), x[..., DIL_ROT:]], axis=-1)


def modulation(c, w, b):
    return (jax.nn.silu(c) @ w + b)[:, None, :]


def ada(x, shift, scale):
    return rms(x) * (1.0 + scale) + shift


def mla_project(h, pos, w_dq, g_cq, w_uq, w_dkv, g_ckv, g_qn, g_qr, g_kr):
    B, T, _ = h.shape
    cq = rms(h @ w_dq) * g_cq
    q = (cq @ w_uq).reshape(B, T, MLA_HEADS, MLA_NOPE + MLA_ROPE)
    q = jnp.concatenate([rms(q[..., :MLA_NOPE]) * g_qn,
                         rope(rms(q[..., MLA_NOPE:]) * g_qr, pos)], axis=-1)
    kv = h @ w_dkv
    ckv = rms(kv[..., :MLA_KV_LORA]) * g_ckv
    k_pe = rope(rms(kv[..., MLA_KV_LORA:]) * g_kr, pos)
    return q, jnp.concatenate([ckv, k_pe], axis=-1)


def mla_expand(lat, w_uk, w_uv, g_kn):
    ckv, k_pe = lat[..., :MLA_KV_LORA], lat[..., MLA_KV_LORA:]
    k_nope = rms(jnp.einsum('btr,rhd->bthd', ckv, w_uk)) * g_kn
    k_pe = jnp.broadcast_to(k_pe[:, :, None, :], k_nope.shape[:-1] + (MLA_ROPE,))
    v = jnp.einsum('btr,rhd->bthd', ckv, w_uv)
    return jnp.concatenate([k_nope, k_pe], axis=-1), v


def causal_attend(q, k, v, q_pos, k_pos):
    s = jnp.einsum('bqhd,bkhd->bhqk', q, k, preferred_element_type=jnp.float32) * (q.shape[-1] ** -0.5)
    s = jnp.where(k_pos[None, :] <= q_pos[:, None], s, NEG)
    p = jax.nn.softmax(s, axis=-1).astype(v.dtype)
    return jnp.einsum('bhqk,bkhd->bqhd', p, v)


def mla_attend_prompt(q, lat, pos, w_uk, w_uv, g_kn):
    B, S = q.shape[:2]
    k, v = mla_expand(lat, w_uk, w_uv, g_kn)
    nb = S // Q_BLOCK
    qb = jnp.moveaxis(q.reshape(B, nb, Q_BLOCK, MLA_HEADS, q.shape[-1]), 1, 0)
    pb = pos.reshape(nb, Q_BLOCK)
    out = lax.map(lambda a: causal_attend(a[0], k, v, a[1], pos), (qb, pb))
    return jnp.moveaxis(out, 0, 1).reshape(B, S, MLA_HEADS * MLA_V)


def mla_attend_sample(q, lat_new, pool, page_table, w_uk, w_uv, g_kn):
    DB, T = q.shape[:2]
    past = page_table.shape[1] * PAGE_SIZE
    k_pos = jnp.arange(past + T, dtype=jnp.int32)
    q_pos = past + jnp.arange(T, dtype=jnp.int32)

    def one(a):
        pt, qs, ln = a
        lat = jnp.concatenate([pool[pt].reshape(past, MLA_LAT), ln], axis=0)[None]
        k, v = mla_expand(lat, w_uk, w_uv, g_kn)
        return causal_attend(qs[None], k, v, q_pos, k_pos)[0]

    out = lax.map(one, (page_table, q, lat_new))
    return out.reshape(DB, T, MLA_HEADS * MLA_V)


def mla_sublayer(x, c, pos, attend, w_mod, b_mod, proj, w_o):
    shift, scale, gate = jnp.split(modulation(c, w_mod, b_mod), 3, axis=-1)
    q, lat = mla_project(ada(x, shift, scale), pos, *proj)
    return x + gate * (attend(q, lat) @ w_o), lat


def shared_kv(s, c, pos, w_mod, b_mod, w_kv, g_k):
    B, T, _ = s.shape
    shift, scale = jnp.split(modulation(c, w_mod, b_mod), 2, axis=-1)
    kv = (ada(s, shift, scale) @ w_kv).reshape(B, T, 2, N_GROUPS, DIL_HEADS, DIL_HEAD_DIM)
    k = partial_rope(rms(kv[:, :, 0]) * g_k[:, None, :], pos)
    return k, kv[:, :, 1]


def dilated_prompt(q, k, v, dil, span):
    B, S, H, Dh = q.shape
    n_sub = S // dil
    n_blk = -(-n_sub // span)
    tail = n_blk * span - n_sub

    def to_sub(a, lead):
        a = a.reshape(B, n_sub, dil, H, Dh).transpose(0, 2, 1, 3, 4)
        return jnp.pad(a, ((0, 0), (0, 0), (lead, tail), (0, 0), (0, 0)))

    qs = to_sub(q, 0).reshape(B, dil, n_blk, span, H, Dh)
    ks = to_sub(k, span).reshape(B, dil, n_blk + 1, span, H, Dh)
    vs = to_sub(v, span).reshape(B, dil, n_blk + 1, span, H, Dh)
    kb = jnp.concatenate([ks[:, :, :-1], ks[:, :, 1:]], axis=3)
    vb = jnp.concatenate([vs[:, :, :-1], vs[:, :, 1:]], axis=3)
    s = jnp.einsum('brnqhd,brnkhd->brnhqk', qs, kb, preferred_element_type=jnp.float32) * (Dh ** -0.5)
    qi = jnp.arange(span)[:, None] + span
    ki = jnp.arange(2 * span)[None, :]
    band = (qi - ki >= 0) & (qi - ki <= span)
    real = (jnp.arange(n_blk)[:, None, None] > 0) | (ki[None] >= span)
    s = jnp.where((band[None] & real)[:, None], s, NEG)
    m = jnp.max(s, axis=-1)
    p = jnp.exp(s - m[..., None])
    den = jnp.sum(p, axis=-1)
    num = jnp.einsum('brnhqk,brnkhd->brnqhd', p, vb, preferred_element_type=jnp.float32)
    num = num.reshape(B, dil, n_blk * span, H, Dh)[:, :, :n_sub].transpose(0, 2, 1, 3, 4).reshape(B, S, H, Dh)

    def back(a):
        a = jnp.swapaxes(a, -1, -2).reshape(B, dil, n_blk * span, H)[:, :, :n_sub]
        return a.transpose(0, 2, 1, 3).reshape(B, S, H)

    return num, back(m), back(den)


def dilated_sample(q, k_src, v_src, dil, span):
    T, Ls, Dh = q.shape[1], k_src.shape[1], q.shape[-1]
    q_idx = Ls - T + jnp.arange(T)
    kidx = q_idx[:, None] - dil * jnp.arange(span + 1)[None, :]
    valid = kidx >= 0
    kidx = jnp.maximum(kidx, 0)
    kg, vg = k_src[:, kidx], v_src[:, kidx]
    s = jnp.einsum('bqhd,bqjhd->bhqj', q, kg, preferred_element_type=jnp.float32) * (Dh ** -0.5)
    s = jnp.where(valid[None, None], s, NEG)
    m = jnp.max(s, axis=-1)
    p = jnp.exp(s - m[..., None])
    den = jnp.sum(p, axis=-1)
    num = jnp.einsum('bhqj,bqjhd->bqhd', p, vg, preferred_element_type=jnp.float32)
    return num, m.transpose(0, 2, 1), den.transpose(0, 2, 1)


def combine_groups(parts):
    big = parts[0][1]
    for _, m, _ in parts[1:]:
        big = jnp.maximum(big, m)
    num = 0.0
    den = 0.0
    for n_g, m_g, d_g in parts:
        w = jnp.exp(m_g - big)
        num = num + w[..., None] * n_g
        den = den + w * d_g
    return num / den[..., None]


def dil_sublayer(x, c, pos, attend_group, w_mod, b_mod, w_q, g_q, w_o):
    B, T, _ = x.shape
    shift, scale, gate = jnp.split(modulation(c, w_mod, b_mod), 3, axis=-1)
    q = (ada(x, shift, scale) @ w_q).reshape(B, T, N_GROUPS, DIL_HEADS, DIL_HEAD_DIM)
    q = partial_rope(rms(q) * g_q[:, None, :], pos)
    parts = [attend_group(g, q[:, :, g]) for g in range(N_GROUPS)]
    o = combine_groups(parts).astype(x.dtype).reshape(B, T, DIL_HEADS * DIL_HEAD_DIM)
    return x + gate * (o @ w_o)


def peer(h, w_q, subkeys, u_tab, v_tab):
    shape = h.shape
    x = h.reshape(-1, shape[-1])
    n = x.shape[0]
    nb = -(-n // PEER_BLOCK)
    x = jnp.pad(x, ((0, nb * PEER_BLOCK - n), (0, 0)))

    def block(xb):
        q = (xb @ w_q).reshape(-1, PEER_HEADS, 2, PEER_DKEY // 2)
        s = jnp.einsum('thpd,hpkd->thpk', q, subkeys, preferred_element_type=jnp.float32)
        sv, si = lax.top_k(s, PEER_TOPK)
        comb = (sv[:, :, 0, :, None] + sv[:, :, 1, None, :]).reshape(-1, PEER_HEADS, PEER_TOPK * PEER_TOPK)
        cid = (si[:, :, 0, :, None] * PEER_NKEYS + si[:, :, 1, None, :]).reshape(-1, PEER_HEADS, PEER_TOPK * PEER_TOPK)
        top_s, top_j = lax.top_k(comb, PEER_TOPK)
        eid = jnp.take_along_axis(cid, top_j, axis=-1)
        g = jax.nn.softmax(top_s, axis=-1)
        a = jax.nn.gelu(jnp.einsum('td,thkd->thk', xb, u_tab[eid]), approximate=False)
        return jnp.einsum('thk,thkd->td', (g * a).astype(xb.dtype), v_tab[eid])

    y = lax.map(block, x.reshape(nb, PEER_BLOCK, shape[-1]))
    return y.reshape(-1, shape[-1])[:n].reshape(shape)


def peer_sublayer(x, c, w_mod, b_mod, w_q, subkeys, u_tab, v_tab):
    shift, scale, gate = jnp.split(modulation(c, w_mod, b_mod), 3, axis=-1)
    return x + gate * peer(ada(x, shift, scale), w_q, subkeys, u_tab, v_tab)


def setup_inputs(seed: int = 0) -> dict:
    key = jax.random.key(seed)
    keys = list(jax.random.split(key, 48))

    def nrm(shape, scale):
        return jax.random.normal(keys.pop(), shape, jnp.float32) * scale

    def gain(shape):
        return 1.0 + nrm(shape, 0.05)

    D = D_MODEL
    n_pages = PAST_LEN // PAGE_SIZE
    n_used = DEC_BATCH * n_pages
    n_pool = n_used + (n_used + 3) // 4
    page_table = jax.random.permutation(keys.pop(), n_pool)[:n_used].reshape(DEC_BATCH, n_pages).astype(jnp.int32)
    gw = N_GROUPS * DIL_HEADS * DIL_HEAD_DIM
    mod = 0.5 * D ** -0.5
    NA, NB = N_A_LAYERS, N_B_LAYERS
    return {
        'x_prompt': nrm((BATCH, SEQ, D), 1.0),
        'x_sample': nrm((DEC_BATCH, DEC_SEQ, D), 1.0),
        'c_prompt': nrm((BATCH, D), 1.0),
        'c_sample': nrm((DEC_BATCH, D), 1.0),
        'cache_mla': nrm((NA, n_pool, PAGE_SIZE, MLA_LAT), 1.0),
        'cache_dil0': nrm((DEC_BATCH, min(DIL_WINDOWS[0], PAST_LEN), 2, DIL_HEADS, DIL_HEAD_DIM), 1.0),
        'cache_dil1': nrm((DEC_BATCH, min(DIL_WINDOWS[1], PAST_LEN), 2, DIL_HEADS, DIL_HEAD_DIM), 1.0),
        'cache_dil2': nrm((DEC_BATCH, min(DIL_WINDOWS[2], PAST_LEN), 2, DIL_HEADS, DIL_HEAD_DIM), 1.0),
        'page_table': page_table,
        'a_mod_w': nrm((NA, D, 3 * D), mod),
        'a_mod_b': nrm((NA, 3 * D), 0.02),
        'a_w_dq': nrm((NA, D, MLA_Q_LORA), D ** -0.5),
        'a_g_cq': gain((NA, MLA_Q_LORA)),
        'a_w_uq': nrm((NA, MLA_Q_LORA, MLA_HEADS * (MLA_NOPE + MLA_ROPE)), MLA_Q_LORA ** -0.5),
        'a_w_dkv': nrm((NA, D, MLA_LAT), D ** -0.5),
        'a_g_ckv': gain((NA, MLA_KV_LORA)),
        'a_g_qn': gain((NA, MLA_NOPE)),
        'a_g_qr': gain((NA, MLA_ROPE)),
        'a_g_kr': gain((NA, MLA_ROPE)),
        'a_w_uk': nrm((NA, MLA_KV_LORA, MLA_HEADS, MLA_NOPE), MLA_KV_LORA ** -0.5),
        'a_g_kn': gain((NA, MLA_NOPE)),
        'a_w_uv': nrm((NA, MLA_KV_LORA, MLA_HEADS, MLA_V), MLA_KV_LORA ** -0.5),
        'a_w_o': nrm((NA, MLA_HEADS * MLA_V, D), (MLA_HEADS * MLA_V) ** -0.5),
        'kv_mod_w': nrm((D, 2 * D), mod),
        'kv_mod_b': nrm((2 * D,), 0.02),
        'kv_w': nrm((D, 2 * gw), D ** -0.5),
        'kv_g_k': gain((N_GROUPS, DIL_HEAD_DIM)),
        'b_mod_w': nrm((NB, D, 3 * D), mod),
        'b_mod_b': nrm((NB, 3 * D), 0.02),
        'b_w_q': nrm((NB, D, gw), D ** -0.5),
        'b_g_q': gain((NB, N_GROUPS, DIL_HEAD_DIM)),
        'b_w_o': nrm((NB, DIL_HEADS * DIL_HEAD_DIM, D), (DIL_HEADS * DIL_HEAD_DIM) ** -0.5),
        'f_mod_w': nrm((DEPTH, D, 3 * D), mod),
        'f_mod_b': nrm((DEPTH, 3 * D), 0.02),
        'f_w_q': nrm((DEPTH, D, PEER_HEADS * PEER_DKEY), D ** -0.5),
        'f_subkeys': nrm((DEPTH, PEER_HEADS, 2, PEER_NKEYS, PEER_DKEY // 2), (PEER_DKEY // 2) ** -0.5),
        'f_u': nrm((DEPTH, PEER_EXPERTS, D), D ** -0.5),
        'f_v': nrm((DEPTH, PEER_EXPERTS, D), PEER_HEADS ** -0.5),
    }


def reference(x_prompt, x_sample, c_prompt, c_sample, cache_mla, cache_dil0, cache_dil1, cache_dil2,
              page_table, a_mod_w, a_mod_b, a_w_dq, a_g_cq, a_w_uq, a_w_dkv, a_g_ckv, a_g_qn, a_g_qr,
              a_g_kr, a_w_uk, a_g_kn, a_w_uv, a_w_o, kv_mod_w, kv_mod_b, kv_w, kv_g_k, b_mod_w, b_mod_b,
              b_w_q, b_g_q, b_w_o, f_mod_w, f_mod_b, f_w_q, f_subkeys, f_u, f_v):
    caches_dil = (cache_dil0, cache_dil1, cache_dil2)
    seq = x_prompt.shape[1]
    dec = x_sample.shape[1]
    past = page_table.shape[1] * PAGE_SIZE
    pos_p = jnp.arange(seq, dtype=jnp.int32)
    pos_s = past + jnp.arange(dec, dtype=jnp.int32)
    xp, xs = x_prompt, x_sample
    rows_p, rows_s = [], []
    dil_p, dil_s, src_s = [], [], []
    kp = vp = None
    for layer in range(DEPTH):
        if layer < N_A_LAYERS:
            i = layer
            proj = (a_w_dq[i], a_g_cq[i], a_w_uq[i], a_w_dkv[i], a_g_ckv[i], a_g_qn[i], a_g_qr[i], a_g_kr[i])
            expand = (a_w_uk[i], a_w_uv[i], a_g_kn[i])
            xp, lat = mla_sublayer(xp, c_prompt, pos_p,
                                   lambda q, l: mla_attend_prompt(q, l, pos_p, *expand),
                                   a_mod_w[i], a_mod_b[i], proj, a_w_o[i])
            rows_p.append(lat)
            xs, lat = mla_sublayer(xs, c_sample, pos_s,
                                   lambda q, l: mla_attend_sample(q, l, cache_mla[i], page_table, *expand),
                                   a_mod_w[i], a_mod_b[i], proj, a_w_o[i])
            rows_s.append(lat)
        else:
            if layer == N_A_LAYERS:
                kp, vp = shared_kv(xp, c_prompt, pos_p, kv_mod_w, kv_mod_b, kv_w, kv_g_k)
                kn, vn = shared_kv(xs, c_sample, pos_s, kv_mod_w, kv_mod_b, kv_w, kv_g_k)
                for g in range(N_GROUPS):
                    w = DIL_WINDOWS[g]
                    new_p = jnp.stack([kp[:, :, g], vp[:, :, g]], axis=2)
                    full_s = jnp.concatenate([caches_dil[g], jnp.stack([kn[:, :, g], vn[:, :, g]], axis=2)], axis=1)
                    src_s.append(full_s)
                    dil_p.append(new_p[:, -min(w, seq):])
                    dil_s.append(full_s[:, -min(w, past + dec):])
            j = layer - N_A_LAYERS
            att_p = lambda g, qg: dilated_prompt(qg, kp[:, :, g], vp[:, :, g], DIL_RATES[g],
                                                 DIL_WINDOWS[g] // DIL_RATES[g])
            att_s = lambda g, qg: dilated_sample(qg, src_s[g][:, :, 0], src_s[g][:, :, 1], DIL_RATES[g],
                                                 DIL_WINDOWS[g] // DIL_RATES[g])
            xp = dil_sublayer(xp, c_prompt, pos_p, att_p, b_mod_w[j], b_mod_b[j], b_w_q[j], b_g_q[j], b_w_o[j])
            xs = dil_sublayer(xs, c_sample, pos_s, att_s, b_mod_w[j], b_mod_b[j], b_w_q[j], b_g_q[j], b_w_o[j])
        xp = peer_sublayer(xp, c_prompt, f_mod_w[layer], f_mod_b[layer], f_w_q[layer], f_subkeys[layer],
                           f_u[layer], f_v[layer])
        xs = peer_sublayer(xs, c_sample, f_mod_w[layer], f_mod_b[layer], f_w_q[layer], f_subkeys[layer],
                           f_u[layer], f_v[layer])
    mla_p = jnp.stack(rows_p)
    mla_s = jnp.stack(rows_s)
    return (xp, xs, mla_p, mla_s, dil_p[0], dil_s[0], dil_p[1], dil_s[1], dil_p[2], dil_s[2])
```

```python
import functools

import numpy as np
import jax
import jax.numpy as jnp
from jax import lax
from jax.experimental import pallas as pl
from jax.experimental.pallas import tpu as pltpu

F32 = jnp.float32
BF16 = jnp.bfloat16
I32 = jnp.int32

LANES = 128
VMEM_LIMIT = 56 * 1024 * 1024

ROPE_THETA = 500000.0
EPS = 1e-6
NEG = -1e30

MLA_HEADS = 16
MLA_NOPE = 64
MLA_ROPE = 32
MLA_V = 64
MLA_KV_LORA = 256
PAGE_SIZE = 128

DIL_WINDOWS = (128, 512, 2048)
DIL_RATES = (1, 4, 16)
N_GROUPS = 3
DIL_HEADS = 8
DIL_HEAD_DIM = 128
DIL_ROT = DIL_HEAD_DIM // 4

PEER_HEADS = 8
PEER_NKEYS = 128
PEER_TOPK = 16
PEER_DKEY = 128


def _cparams(sem):
    return pltpu.CompilerParams(dimension_semantics=sem, vmem_limit_bytes=VMEM_LIMIT)


def _rms(x):
    return x * lax.rsqrt(jnp.mean(x * x, axis=-1, keepdims=True) + EPS)


def _dot(a, b):
    return jnp.dot(a, b, preferred_element_type=F32)


def _dot_nt(a, b):
    return lax.dot_general(a, b, (((1,), (1,)), ((), ())), preferred_element_type=F32)


def _mod_specs(per_row, tm, d, rows_per_batch):
    if per_row:
        return pl.BlockSpec((tm, d), lambda i: (i, 0))
    return pl.BlockSpec((None, 1, d), lambda i: ((i * tm) // rows_per_batch, 0, 0))


def _mod_kernel(c_ref, w_ref, b_ref, o_ref):
    c = c_ref[...]
    s = (c * jax.nn.sigmoid(c)).astype(BF16)
    o_ref[...] = _dot(s, w_ref[...].astype(BF16)) + b_ref[...]


def modulation(c, w, b):
    bc, d = c.shape
    n = w.shape[1]
    tn = 1024
    return pl.pallas_call(
        _mod_kernel,
        out_shape=jax.ShapeDtypeStruct((bc, n), F32),
        grid=(n // tn,),
        in_specs=[pl.BlockSpec((bc, d), lambda j: (0, 0)),
                  pl.BlockSpec((d, tn), lambda j: (0, j)),
                  pl.BlockSpec((1, tn), lambda j: (0, j))],
        out_specs=pl.BlockSpec((bc, tn), lambda j: (0, j)),
        compiler_params=_cparams(("arbitrary",)),
        name="modulation",
    )(c, w, b.reshape(1, n))


def _rope_tables(pos, lane0, half):
    inv = ROPE_THETA ** (-jnp.arange(half, dtype=F32) / half)
    ang = pos.astype(F32)[:, None] * inv
    cos, sin = jnp.cos(ang), jnp.sin(ang)
    t = pos.shape[0]
    cos_t = jnp.ones((t, LANES), F32)
    sin_t = jnp.zeros((t, LANES), F32)
    cos_t = cos_t.at[:, lane0:lane0 + half].set(cos).at[:, lane0 + half:lane0 + 2 * half].set(cos)
    sin_t = sin_t.at[:, lane0:lane0 + half].set(-sin).at[:, lane0 + half:lane0 + 2 * half].set(sin)
    return cos_t, sin_t


def _rope_slab(s, cos_t, sin_t, lane, lane0, half):
    swapped = jnp.where(lane < lane0 + half, pltpu.roll(s, LANES - half, 1), pltpu.roll(s, half, 1))
    return s * cos_t + swapped * sin_t


MLA_SLAB_ROPE0 = MLA_NOPE


def _mla_proj_kernel(x_ref, sh_ref, sc_ref, cos_ref, sin_ref, wdq_ref, gcq_ref, wuq_ref, wdkv_ref, gckv_ref,
                     gq_ref, gkr_ref, wuk_ref, gkn_ref, wuv_ref,
                     q_ref, k_ref, v_ref, ckv_ref, kpe_ref, *, qscale):
    x = x_ref[...]
    tm = x.shape[0]
    h = _rms(x) * (1.0 + sc_ref[...]) + sh_ref[...]
    hb = h.astype(BF16)
    cq = _rms(_dot(hb, wdq_ref[...])) * gcq_ref[...]
    q = _dot(cq.astype(BF16), wuq_ref[...])
    kv = _dot(hb, wdkv_ref[...])
    ckv = _rms(kv[:, :MLA_KV_LORA]) * gckv_ref[...]
    cos_t, sin_t = cos_ref[...], sin_ref[...]
    lane = lax.broadcasted_iota(I32, (tm, LANES), 1)
    m_n = lane < MLA_NOPE
    m_r = (lane >= MLA_NOPE) & (lane < MLA_NOPE + MLA_ROPE)

    def norm_rope(s, g):
        s2 = s * s
        ssn = jnp.sum(jnp.where(m_n, s2, 0.0), axis=-1, keepdims=True)
        ssr = jnp.sum(jnp.where(m_r, s2, 0.0), axis=-1, keepdims=True)
        r = jnp.where(m_n, lax.rsqrt(ssn / MLA_NOPE + EPS), lax.rsqrt(ssr / MLA_ROPE + EPS))
        return _rope_slab(s * r * g, cos_t, sin_t, lane, MLA_SLAB_ROPE0, MLA_ROPE // 2)

    kpe = norm_rope(kv[:, MLA_KV_LORA:], gkr_ref[...])
    ckv_ref[...] = ckv
    kpe_ref[...] = kpe
    ckvb = ckv.astype(BF16)
    kn = _dot(ckvb, wuk_ref[...])
    v_ref[...] = _dot(ckvb, wuv_ref[...]).astype(BF16)
    gq = gq_ref[...]
    gkn = gkn_ref[...]
    for hh in range(MLA_HEADS):
        sl = slice(hh * LANES, (hh + 1) * LANES)
        q_ref[hh] = (norm_rope(q[:, sl], gq) * qscale).astype(BF16)
        ks = kn[:, sl]
        ssn = jnp.sum(ks * ks, axis=-1, keepdims=True)
        k_ref[hh] = (ks * lax.rsqrt(ssn / MLA_NOPE + EPS) * gkn + kpe).astype(BF16)


def _mla_weights(w_dq, g_cq, w_uq, w_dkv, g_ckv, g_qn, g_qr, g_kr, w_uk, g_kn, w_uv):
    d = w_dq.shape[0]
    hd = MLA_NOPE + MLA_ROPE
    pad = LANES - hd
    wuq = jnp.pad(w_uq.reshape(-1, MLA_HEADS, hd), ((0, 0), (0, 0), (0, pad))).reshape(-1, MLA_HEADS * LANES)
    wdkv = jnp.concatenate([w_dkv[:, :MLA_KV_LORA], jnp.zeros((d, MLA_NOPE), F32), w_dkv[:, MLA_KV_LORA:],
                            jnp.zeros((d, pad), F32)], axis=1)
    gq = jnp.concatenate([g_qn, g_qr, jnp.zeros((pad,), F32)])[None]
    gkr = jnp.concatenate([jnp.zeros((MLA_NOPE,), F32), g_kr, jnp.zeros((pad,), F32)])[None]
    wuk = jnp.pad(w_uk, ((0, 0), (0, 0), (0, LANES - MLA_NOPE))).reshape(MLA_KV_LORA, MLA_HEADS * LANES)
    gkn = jnp.concatenate([g_kn, jnp.zeros((LANES - MLA_NOPE,), F32)])[None]
    wuv = w_uv.reshape(MLA_KV_LORA, MLA_HEADS * MLA_V)
    return dict(wdq=w_dq.astype(BF16), gcq=g_cq[None], wuq=wuq.astype(BF16), wdkv=wdkv.astype(BF16),
                gckv=g_ckv[None], gq=gq, gkr=gkr, wuk=wuk.astype(BF16), gkn=gkn, wuv=wuv.astype(BF16))


def mla_project(x, shift, scale, cos_t, sin_t, w, *, per_row, rows_per_batch, tm):
    n, d = x.shape
    nblk_pos = cos_t.shape[0] // tm
    full = lambda a: pl.BlockSpec(a.shape, lambda i: (0,) * a.ndim)
    wnames = ("wdq", "gcq", "wuq", "wdkv", "gckv", "gq", "gkr", "wuk", "gkn", "wuv")
    qscale = float((MLA_NOPE + MLA_ROPE) ** -0.5)
    return pl.pallas_call(
        functools.partial(_mla_proj_kernel, qscale=qscale),
        out_shape=(jax.ShapeDtypeStruct((MLA_HEADS, n, LANES), BF16),
                   jax.ShapeDtypeStruct((MLA_HEADS, n, LANES), BF16),
                   jax.ShapeDtypeStruct((n, MLA_HEADS * MLA_V), BF16),
                   jax.ShapeDtypeStruct((n, MLA_KV_LORA), F32),
                   jax.ShapeDtypeStruct((n, LANES), F32)),
        grid=(n // tm,),
        in_specs=[pl.BlockSpec((tm, d), lambda i: (i, 0)),
                  _mod_specs(per_row, tm, d, rows_per_batch),
                  _mod_specs(per_row, tm, d, rows_per_batch),
                  pl.BlockSpec((tm, LANES), lambda i: (i % nblk_pos, 0)),
                  pl.BlockSpec((tm, LANES), lambda i: (i % nblk_pos, 0))] + [full(w[k]) for k in wnames],
        out_specs=(pl.BlockSpec((MLA_HEADS, tm, LANES), lambda i: (0, i, 0)),
                   pl.BlockSpec((MLA_HEADS, tm, LANES), lambda i: (0, i, 0)),
                   pl.BlockSpec((tm, MLA_HEADS * MLA_V), lambda i: (i, 0)),
                   pl.BlockSpec((tm, MLA_KV_LORA), lambda i: (i, 0)),
                   pl.BlockSpec((tm, LANES), lambda i: (i, 0))),
        compiler_params=_cparams(("arbitrary",)),
        name="mla_project",
    )(x, shift, scale, cos_t, sin_t, *[w[k] for k in wnames])


def _mla_attn_kernel(q_ref, k_ref, v_ref, o_ref, *, tq):
    qi = pl.program_id(2)
    row = lax.broadcasted_iota(I32, (tq, tq), 0)
    col = lax.broadcasted_iota(I32, (tq, tq), 1)
    outs = []
    for hh in range(2):
        q = q_ref[hh]

        def body(j, carry, hh=hh, q=q):
            m, l, acc = carry
            off = pl.multiple_of(j * tq, tq)
            kc = k_ref[hh, pl.ds(off, tq), :]
            vc = v_ref[pl.ds(off, tq), :]
            s = _dot_nt(q, kc)
            s = jnp.where(col + (j - qi) * tq <= row, s, NEG)
            m_new = jnp.maximum(m, jnp.max(s, axis=-1, keepdims=True))
            alpha = jnp.exp(m - m_new)
            p = jnp.exp(s - m_new)
            l = alpha * l + jnp.sum(p, axis=-1, keepdims=True)
            acc = alpha * acc + _dot(p.astype(BF16), vc)
            return m_new, l, acc

        m0 = jnp.full((tq, 1), NEG, F32)
        l0 = jnp.zeros((tq, 1), F32)
        a0 = jnp.zeros((tq, LANES), F32)
        m, l, acc = lax.fori_loop(0, qi + 1, body, (m0, l0, a0))
        outs.append(acc / l)
    lane = lax.broadcasted_iota(I32, (tq, LANES), 1)
    o_ref[...] = jnp.where(lane < MLA_V, outs[0], outs[1]).astype(BF16)


def mla_attention(q, k, v, b, s):
    tq = 256
    hp = MLA_HEADS // 2
    q4 = q.reshape(MLA_HEADS, b, s, LANES)
    k4 = k.reshape(MLA_HEADS, b, s, LANES)
    v3 = v.reshape(b, s, MLA_HEADS * MLA_V)
    out = pl.pallas_call(
        functools.partial(_mla_attn_kernel, tq=tq),
        out_shape=jax.ShapeDtypeStruct((b, s, MLA_HEADS * MLA_V), BF16),
        grid=(b, hp, s // tq),
        in_specs=[pl.BlockSpec((2, None, tq, LANES), lambda bi, h, qi: (h, bi, qi, 0)),
                  pl.BlockSpec((2, None, s, LANES), lambda bi, h, qi: (h, bi, 0, 0)),
                  pl.BlockSpec((None, s, LANES), lambda bi, h, qi: (bi, 0, h))],
        out_specs=pl.BlockSpec((None, tq, LANES), lambda bi, h, qi: (bi, qi, h)),
        compiler_params=_cparams(("arbitrary", "arbitrary", "arbitrary")),
        name="mla_attention",
    )(q4, k4, v3)
    return out.reshape(b * s, MLA_HEADS * MLA_V)


def _out_proj_kernel(x_ref, o_ref, w_ref, g_ref, y_ref):
    y_ref[...] = x_ref[...] + g_ref[...] * _dot(o_ref[...], w_ref[...])


def out_proj(x, o, w, gate, *, per_row, rows_per_batch, tm):
    n, d = x.shape
    kdim = o.shape[1]
    return pl.pallas_call(
        _out_proj_kernel,
        out_shape=jax.ShapeDtypeStruct((n, d), F32),
        grid=(n // tm,),
        in_specs=[pl.BlockSpec((tm, d), lambda i: (i, 0)),
                  pl.BlockSpec((tm, kdim), lambda i: (i, 0)),
                  pl.BlockSpec((kdim, d), lambda i: (0, 0)),
                  _mod_specs(per_row, tm, d, rows_per_batch)],
        out_specs=pl.BlockSpec((tm, d), lambda i: (i, 0)),
        compiler_params=_cparams(("arbitrary",)),
        name="out_proj",
    )(x, o, w, gate)


MLA_DEC_PAGES = 4
MLA_DEC_ROWS = 8


def _qabs_kernel(q_ref, w_ref, o_ref):
    o_ref[...] = _dot(q_ref[...], w_ref[...]).astype(BF16)


def mla_absorb_q(q, w_uk, g_kn):
    h, n, _ = q.shape
    wt = jnp.transpose(w_uk, (1, 2, 0)) * g_kn[None, :, None]
    wt = jnp.pad(wt, ((0, 0), (0, LANES - MLA_NOPE), (0, 0))).astype(BF16)
    return pl.pallas_call(
        _qabs_kernel,
        out_shape=jax.ShapeDtypeStruct((h, n, MLA_KV_LORA), BF16),
        grid=(h,),
        in_specs=[pl.BlockSpec((None, n, LANES), lambda i: (i, 0, 0)),
                  pl.BlockSpec((None, LANES, MLA_KV_LORA), lambda i: (i, 0, 0))],
        out_specs=pl.BlockSpec((None, n, MLA_KV_LORA), lambda i: (i, 0, 0)),
        compiler_params=_cparams(("arbitrary",)),
        name="mla_absorb_q",
    )(q, wt)


def _mla_decode_kernel(pt_ref, qa_ref, qp_ref, new_ref, wuk_ref, ind_ref, *rest, n_dec):
    page_refs = rest[:MLA_DEC_PAGES]
    o_ref = rest[MLA_DEC_PAGES]
    m_ref, l_ref, acc_ref = rest[MLA_DEC_PAGES + 1:]
    j = pl.program_id(1)
    qa = qa_ref[...]
    qp = qp_ref[...]

    def scores(lat):
        ckvb = lat[:, :MLA_KV_LORA].astype(BF16)
        kpeb = lat[:, MLA_KV_LORA:].astype(BF16)
        kn = _dot(ckvb, wuk_ref[...])
        ssq = _dot((kn * kn).astype(BF16), ind_ref[...])
        r = lax.rsqrt(ssq / MLA_NOPE + EPS)
        return _dot(ckvb, qa) * r + _dot(kpeb, qp), ckvb

    def accumulate(s, ckvb):
        m = m_ref[...]
        m_new = jnp.maximum(m, jnp.max(s, axis=0, keepdims=True))
        alpha = jnp.exp(m - m_new)
        p = jnp.exp(s - m_new)
        l_ref[...] = alpha * l_ref[...] + jnp.sum(p, axis=0, keepdims=True)
        pv = lax.dot_general(ckvb, p.astype(BF16), (((0,), (0,)), ((), ())), preferred_element_type=F32)
        acc_ref[...] = alpha * acc_ref[...] + pv
        m_ref[...] = m_new

    @pl.when(j == 0)
    def _():
        m_ref[...] = jnp.full(m_ref.shape, NEG, F32)
        l_ref[...] = jnp.zeros(l_ref.shape, F32)
        acc_ref[...] = jnp.zeros(acc_ref.shape, F32)
        s, ckvb = scores(new_ref[...])
        key = lax.broadcasted_iota(I32, s.shape, 0)
        tok = lax.broadcasted_iota(I32, s.shape, 1) % n_dec
        accumulate(jnp.where((key <= tok) & (key < n_dec), s, NEG), ckvb)

    lat = jnp.concatenate([r[...] for r in page_refs], axis=0)
    s, ckvb = scores(lat)
    accumulate(s, ckvb)

    @pl.when(j == pl.num_programs(1) - 1)
    def _():
        o_ref[...] = acc_ref[...] / l_ref[...]


def mla_decode_attention(qabs_t, qpe_t, lat_new, pool, page_table, w_uk, n_dec):
    db, n_pages = page_table.shape
    lat = pool.shape[-1]
    wuk = w_uk.reshape(MLA_KV_LORA, MLA_HEADS * MLA_NOPE).astype(BF16)
    col = np.arange(LANES)
    ind = (np.arange(MLA_HEADS * MLA_NOPE)[:, None] // MLA_NOPE == (col // n_dec)[None, :]) & (col[None, :] < MLA_HEADS * n_dec)
    ind = jnp.asarray(ind, BF16)
    steps = n_pages // MLA_DEC_PAGES
    page_specs = [pl.BlockSpec((None, PAGE_SIZE, lat),
                               lambda b, j, pt, pp=pp: (pt[b * n_pages + j * MLA_DEC_PAGES + pp], 0, 0))
                  for pp in range(MLA_DEC_PAGES)]
    grid_spec = pltpu.PrefetchScalarGridSpec(
        num_scalar_prefetch=1,
        grid=(db, steps),
        in_specs=[pl.BlockSpec((None, MLA_KV_LORA, LANES), lambda b, j, pt: (b, 0, 0)),
                  pl.BlockSpec((None, MLA_ROPE, LANES), lambda b, j, pt: (b, 0, 0)),
                  pl.BlockSpec((None, MLA_DEC_ROWS, lat), lambda b, j, pt: (b, 0, 0)),
                  pl.BlockSpec(wuk.shape, lambda b, j, pt: (0, 0)),
                  pl.BlockSpec(ind.shape, lambda b, j, pt: (0, 0))] + page_specs,
        out_specs=pl.BlockSpec((None, MLA_KV_LORA, LANES), lambda b, j, pt: (b, 0, 0)),
        scratch_shapes=[pltpu.VMEM((1, LANES), F32), pltpu.VMEM((1, LANES), F32),
                        pltpu.VMEM((MLA_KV_LORA, LANES), F32)],
    )
    return pl.pallas_call(
        functools.partial(_mla_decode_kernel, n_dec=n_dec),
        out_shape=jax.ShapeDtypeStruct((db, MLA_KV_LORA, LANES), F32),
        grid_spec=grid_spec,
        compiler_params=_cparams(("arbitrary", "arbitrary")),
        name="mla_decode_attention",
    )(page_table.reshape(-1), qabs_t, qpe_t, lat_new, wuk, ind, *([pool] * MLA_DEC_PAGES))


def _mla_uv_kernel(x_ref, w_ref, o_ref):
    w = w_ref[...]
    lane = lax.broadcasted_iota(I32, o_ref.shape, 1)
    o_ref[...] = jnp.where(lane < MLA_V, _dot(x_ref[0], w), _dot(x_ref[1], w)).astype(BF16)


def mla_decode_values(x, w_uv):
    h, n, _ = x.shape
    wuv = w_uv.reshape(MLA_KV_LORA, MLA_HEADS * MLA_V).astype(BF16)
    return pl.pallas_call(
        _mla_uv_kernel,
        out_shape=jax.ShapeDtypeStruct((n, MLA_HEADS * MLA_V), BF16),
        grid=(h // 2,),
        in_specs=[pl.BlockSpec((2, n, MLA_KV_LORA), lambda i: (i, 0, 0)),
                  pl.BlockSpec((MLA_KV_LORA, LANES), lambda i: (0, i))],
        out_specs=pl.BlockSpec((n, LANES), lambda i: (0, i)),
        compiler_params=_cparams(("arbitrary",)),
        name="mla_decode_values",
    )(x, wuv)


def mla_sample_attention(q, ckv, kpe, pool, page_table, w_uk, g_kn, w_uv, n_dec):
    db = page_table.shape[0]
    n = db * n_dec
    qabs = mla_absorb_q(q, w_uk, g_kn)
    ncol = MLA_HEADS * n_dec

    def to_cols(a):
        f = a.shape[-1]
        a = a.reshape(MLA_HEADS, db, n_dec, f).transpose(1, 3, 0, 2).reshape(db, f, ncol)
        return jnp.pad(a, ((0, 0), (0, 0), (0, LANES - ncol)))

    qabs_t = to_cols(qabs)
    qpe_t = to_cols(q[:, :, MLA_NOPE:MLA_NOPE + MLA_ROPE])
    lat_new = jnp.concatenate([ckv, kpe[:, MLA_NOPE:MLA_NOPE + MLA_ROPE]], axis=1).reshape(db, n_dec, -1)
    lat_new = jnp.pad(lat_new, ((0, 0), (0, MLA_DEC_ROWS - n_dec), (0, 0)))
    out_t = mla_decode_attention(qabs_t, qpe_t, lat_new, pool, page_table, w_uk, n_dec)
    x = out_t[:, :, :ncol].reshape(db, MLA_KV_LORA, MLA_HEADS, n_dec).transpose(2, 0, 3, 1)
    x = x.reshape(MLA_HEADS, n, MLA_KV_LORA).astype(BF16)
    return mla_decode_values(x, w_uv)


def _ada_heads_kernel(x_ref, sh_ref, sc_ref, cos_ref, sin_ref, w_ref, g_ref, *out_refs, n_norm, n_slab, oscale, emit_f32):
    x = x_ref[...]
    tm = x.shape[0]
    h = _rms(x) * (1.0 + sc_ref[...]) + sh_ref[...]
    y = _dot(h.astype(BF16), w_ref[...])
    cos_t, sin_t = cos_ref[...], sin_ref[...]
    g = g_ref[...]
    lane = lax.broadcasted_iota(I32, (tm, LANES), 1)
    ob_ref = out_refs[-1]
    for s in range(n_slab):
        sl = slice(s * LANES, (s + 1) * LANES)
        ys = y[:, sl]
        if s < n_norm:
            ys = _rope_slab(_rms(ys) * g, cos_t, sin_t, lane, 0, DIL_ROT // 2)
        if emit_f32:
            out_refs[0][:, sl] = ys
        ob_ref[:, sl] = (ys * oscale).astype(BF16) if s < n_norm else ys.astype(BF16)


def ada_heads(x, shift, scale, cos_t, sin_t, w, g, *, n_norm, oscale, emit_f32, per_row, rows_per_batch, tm):
    n, d = x.shape
    ng, _, gw = w.shape
    n_slab = gw // LANES
    nblk_pos = cos_t.shape[0] // tm
    mod_spec = (pl.BlockSpec((tm, d), lambda gi, i: (i, 0)) if per_row else
                pl.BlockSpec((None, 1, d), lambda gi, i: ((i * tm) // rows_per_batch, 0, 0)))
    out_shape = [jax.ShapeDtypeStruct((n, ng * gw), BF16)]
    out_specs = [pl.BlockSpec((tm, gw), lambda gi, i: (i, gi))]
    if emit_f32:
        out_shape = [jax.ShapeDtypeStruct((n, ng * gw), F32)] + out_shape
        out_specs = [pl.BlockSpec((tm, gw), lambda gi, i: (i, gi))] + out_specs
    return pl.pallas_call(
        functools.partial(_ada_heads_kernel, n_norm=n_norm, n_slab=n_slab, oscale=oscale, emit_f32=emit_f32),
        out_shape=tuple(out_shape),
        grid=(ng, n // tm),
        in_specs=[pl.BlockSpec((tm, d), lambda gi, i: (i, 0)), mod_spec, mod_spec,
                  pl.BlockSpec((tm, LANES), lambda gi, i: (i % nblk_pos, 0)),
                  pl.BlockSpec((tm, LANES), lambda gi, i: (i % nblk_pos, 0)),
                  pl.BlockSpec((None, d, gw), lambda gi, i: (gi, 0, 0)),
                  pl.BlockSpec((None, 1, LANES), lambda gi, i: (gi, 0, 0))],
        out_specs=tuple(out_specs),
        compiler_params=_cparams(("arbitrary", "arbitrary")),
        name="ada_heads",
    )(x, shift, scale, cos_t, sin_t, w, g)


DIL_SPAN = 128
STAT_DEN0 = DIL_HEADS


def _dil_prompt_kernel(q_ref, kp_ref, vp_ref, kc_ref, vc_ref, num_ref, st_ref):
    nb = pl.program_id(2)
    span = DIL_SPAN
    iq = lax.broadcasted_iota(I32, (span, span), 0)
    jk = lax.broadcasted_iota(I32, (span, span), 1)
    mask_prev = (jk >= iq) & (nb > 0)
    mask_cur = jk <= iq
    lane = lax.broadcasted_iota(I32, (span, LANES), 1)
    stats = jnp.zeros((span, LANES), F32)
    for hh in range(DIL_HEADS):
        sl = slice(hh * LANES, (hh + 1) * LANES)
        q = q_ref[:, sl]
        sp = jnp.where(mask_prev, _dot_nt(q, kp_ref[:, sl]), NEG)
        sc = jnp.where(mask_cur, _dot_nt(q, kc_ref[:, sl]), NEG)
        m = jnp.maximum(jnp.max(sp, axis=-1, keepdims=True), jnp.max(sc, axis=-1, keepdims=True))
        pp = jnp.exp(sp - m)
        pc = jnp.exp(sc - m)
        den = jnp.sum(pp, axis=-1, keepdims=True) + jnp.sum(pc, axis=-1, keepdims=True)
        num_ref[:, sl] = _dot(pp.astype(BF16), vp_ref[:, sl]) + _dot(pc.astype(BF16), vc_ref[:, sl])
        stats = jnp.where(lane == hh, m, stats)
        stats = jnp.where(lane == STAT_DEN0 + hh, den, stats)
    st_ref[...] = stats


def dil_prompt_attention(q, kv, g, b, s):
    dil = DIL_RATES[g]
    span = DIL_WINDOWS[g] // dil
    assert span == DIL_SPAN and s % (dil * span) == 0
    n_sub = s // dil
    n_blk = n_sub // span
    hw = DIL_HEADS * DIL_HEAD_DIM
    qw, kw = q.shape[1], kv.shape[1]
    qv = q.reshape(b, n_sub, dil * qw)
    kvv = kv.reshape(b, n_sub, dil * kw)
    qb, kb = qw // hw, kw // hw
    prev = lambda nb: jnp.maximum(nb - 1, 0)
    num, st = pl.pallas_call(
        _dil_prompt_kernel,
        out_shape=(jax.ShapeDtypeStruct((b, n_sub, dil * hw), F32),
                   jax.ShapeDtypeStruct((b, n_sub, dil * LANES), F32)),
        grid=(b, dil, n_blk),
        in_specs=[pl.BlockSpec((None, span, hw), lambda bi, r, nb: (bi, nb, r * qb + g)),
                  pl.BlockSpec((None, span, hw), lambda bi, r, nb: (bi, prev(nb), r * kb + 2 * g)),
                  pl.BlockSpec((None, span, hw), lambda bi, r, nb: (bi, prev(nb), r * kb + 2 * g + 1)),
                  pl.BlockSpec((None, span, hw), lambda bi, r, nb: (bi, nb, r * kb + 2 * g)),
                  pl.BlockSpec((None, span, hw), lambda bi, r, nb: (bi, nb, r * kb + 2 * g + 1))],
        out_specs=(pl.BlockSpec((None, span, hw), lambda bi, r, nb: (bi, nb, r)),
                   pl.BlockSpec((None, span, LANES), lambda bi, r, nb: (bi, nb, r))),
        compiler_params=_cparams(("arbitrary", "arbitrary", "arbitrary")),
        name=f"dil_prompt_attention_g{g}",
    )(qv, kvv, kvv, kvv, kvv)
    return num.reshape(b * s, hw), st.reshape(b * s, LANES)


def _dil_sample_kernel(q_ref, cache_ref, new_ref, num_ref, st_ref, *, dil, n_dec):
    t = pl.program_id(1)
    hw = DIL_HEADS * DIL_HEAD_DIM
    rows = cache_ref.shape[0]
    n_new = new_ref.shape[0]
    crow = lax.broadcasted_iota(I32, (rows, 1), 0)
    nrow = lax.broadcasted_iota(I32, (n_new, 1), 0)
    cache_ok = (crow >= t) if dil == 1 else (crow >= 0)
    new_ok = (nrow <= t) & (nrow < n_dec) if dil == 1 else (nrow == t)
    lane = lax.broadcasted_iota(I32, (1, LANES), 1)
    stats = jnp.zeros((1, LANES), F32)
    for hh in range(DIL_HEADS):
        sl = slice(hh * LANES, (hh + 1) * LANES)
        q = q_ref[:, sl].astype(F32)
        sc = jnp.where(cache_ok, jnp.sum(cache_ref[:, sl] * q, axis=-1, keepdims=True), NEG)
        sn = jnp.where(new_ok, jnp.sum(new_ref[:, sl] * q, axis=-1, keepdims=True), NEG)
        m = jnp.maximum(jnp.max(sc, axis=0, keepdims=True), jnp.max(sn, axis=0, keepdims=True))
        pc = jnp.exp(sc - m)
        pn = jnp.exp(sn - m)
        den = jnp.sum(pc, axis=0, keepdims=True) + jnp.sum(pn, axis=0, keepdims=True)
        vs = slice(hw + hh * LANES, hw + (hh + 1) * LANES)
        num_ref[:, sl] = (jnp.sum(pc * cache_ref[:, vs], axis=0, keepdims=True)
                          + jnp.sum(pn * new_ref[:, vs], axis=0, keepdims=True))
        stats = jnp.where(lane == hh, m, stats)
        stats = jnp.where(lane == STAT_DEN0 + hh, den, stats)
    st_ref[...] = stats


def dil_sample_attention(q, cache, kv_new, g, n_dec):
    dil = DIL_RATES[g]
    db, w = cache.shape[:2]
    assert w == DIL_WINDOWS[g] and w // dil == DIL_SPAN and (dil == 1 or n_dec <= dil)
    hw = DIL_HEADS * DIL_HEAD_DIM
    rows = w // dil
    cv = cache.reshape(db, rows, dil * 2 * hw)
    q4 = q.reshape(db, n_dec, 1, q.shape[1])
    n_new = kv_new.shape[1]
    num, st = pl.pallas_call(
        functools.partial(_dil_sample_kernel, dil=dil, n_dec=n_dec),
        out_shape=(jax.ShapeDtypeStruct((db, n_dec, 1, hw), F32),
                   jax.ShapeDtypeStruct((db, n_dec, 1, LANES), F32)),
        grid=(db, n_dec),
        in_specs=[pl.BlockSpec((None, None, 1, hw), lambda bi, t: (bi, t, 0, g)),
                  pl.BlockSpec((None, rows, 2 * hw), lambda bi, t: (bi, 0, t if dil > 1 else 0)),
                  pl.BlockSpec((None, n_new, 2 * hw), lambda bi, t: (bi, 0, g))],
        out_specs=(pl.BlockSpec((None, None, 1, hw), lambda bi, t: (bi, t, 0, 0)),
                   pl.BlockSpec((None, None, 1, LANES), lambda bi, t: (bi, t, 0, 0))),
        compiler_params=_cparams(("arbitrary", "arbitrary")),
        name=f"dil_sample_attention_g{g}",
    )(q4, cv, kv_new)
    return num.reshape(db * n_dec, hw), st.reshape(db * n_dec, LANES)


def _dil_combine_kernel(x_ref, n0_ref, n1_ref, n2_ref, s0_ref, s1_ref, s2_ref, w_ref, g_ref, y_ref):
    nums = (n0_ref, n1_ref, n2_ref)
    stats = (s0_ref[...], s1_ref[...], s2_ref[...])
    outs = []
    for hh in range(DIL_HEADS):
        sl = slice(hh * LANES, (hh + 1) * LANES)
        ms = [st[:, hh:hh + 1] for st in stats]
        ds = [st[:, STAT_DEN0 + hh:STAT_DEN0 + hh + 1] for st in stats]
        big = jnp.maximum(jnp.maximum(ms[0], ms[1]), ms[2])
        num = 0.0
        den = 0.0
        for gi in range(N_GROUPS):
            wgt = jnp.exp(ms[gi] - big)
            num = num + wgt * nums[gi][:, sl]
            den = den + wgt * ds[gi]
        outs.append((num / den).astype(BF16))
    o = jnp.concatenate(outs, axis=1)
    y_ref[...] = x_ref[...] + g_ref[...] * _dot(o, w_ref[...])


def dil_combine(x, nums, stats, w, gate, *, per_row, rows_per_batch, tm):
    n, d = x.shape
    hw = DIL_HEADS * DIL_HEAD_DIM
    row = lambda width: pl.BlockSpec((tm, width), lambda i: (i, 0))
    return pl.pallas_call(
        _dil_combine_kernel,
        out_shape=jax.ShapeDtypeStruct((n, d), F32),
        grid=(n // tm,),
        in_specs=[row(d)] + [row(hw)] * 3 + [row(LANES)] * 3 + [pl.BlockSpec((hw, d), lambda i: (0, 0)),
                                                                 _mod_specs(per_row, tm, d, rows_per_batch)],
        out_specs=row(d),
        compiler_params=_cparams(("arbitrary",)),
        name="dil_combine",
    )(x, *nums, *stats, w, gate)


PEER_CNT = tuple(PEER_TOPK // (a + 1) for a in range(PEER_TOPK))
PEER_OFF = tuple(int(v) for v in np.cumsum((0,) + PEER_CNT[:-1]))
PEER_ZSLOT = LANES - 1
PEER_OC_SHIFT = 6
NEG_INF = float("-inf")


def _lookup16(idx, table):
    out = jnp.zeros_like(idx)
    for a, v in enumerate(table):
        if v:
            out = jnp.where(idx == a, v, out)
    return out


def _peer_route_kernel(x_ref, sh_ref, sc_ref, wq_ref, sk_ref,
                       hb_ref, g_ref, i_ref, j_ref, s_ref, li_ref, lj_ref,
                       st_ref, sv_ref, si_ref):
    k = PEER_TOPK
    nk = PEER_NKEYS
    x = x_ref[...]
    tm = x.shape[0]
    nchunk = tm // LANES
    h = _rms(x) * (1.0 + sc_ref[...]) + sh_ref[...]
    hb = h.astype(BF16)
    hb_ref[...] = hb
    q = _dot(hb, wq_ref[...]).astype(BF16)
    st = _dot_nt(sk_ref[...], q)
    for c in range(nchunk):
        st_ref[c] = st[:, c * LANES:(c + 1) * LANES]
    rk = lax.broadcasted_iota(I32, (nk, LANES), 0)
    r16 = lax.broadcasted_iota(I32, (k, LANES), 0)
    r8 = lax.broadcasted_iota(I32, (8, LANES), 0)

    def stage1(t, _):
        c = t // (2 * PEER_HEADS)
        hp = t % (2 * PEER_HEADS)
        s = st_ref[c, pl.ds(pl.multiple_of(hp * nk, nk), nk), :]
        sv = jnp.zeros((k, LANES), F32)
        si = jnp.zeros((k, LANES), I32)
        for it in range(k):
            m = jnp.max(s, axis=0, keepdims=True)
            idx = jnp.min(jnp.where(s == m, rk, nk), axis=0, keepdims=True)
            s = jnp.where(rk == idx, NEG_INF, s)
            sv = jnp.where(r16 == it, m, sv)
            si = jnp.where(r16 == it, idx, si)
        sv_ref[c, pl.ds(pl.multiple_of(hp * k, k), k), :] = sv
        si_ref[c, pl.ds(pl.multiple_of(hp * k, k), k), :] = si
        return 0

    lax.fori_loop(0, nchunk * 2 * PEER_HEADS, stage1, 0)

    def stage2(t, _):
        c = t // PEER_HEADS
        hh = t % PEER_HEADS
        o0 = pl.multiple_of(hh * 2 * k, 2 * k)
        o1 = pl.multiple_of(hh * 2 * k + k, k)
        sv0 = sv_ref[c, pl.ds(o0, k), :]
        sv1 = sv_ref[c, pl.ds(o1, k), :]
        si0 = si_ref[c, pl.ds(o0, k), :]
        si1 = si_ref[c, pl.ds(o1, k), :]
        tiles, codes = [], []
        for a in range(8):
            for b0 in range(0, PEER_CNT[a], 8):
                tile = sv0[a:a + 1, :] + sv1[b0:b0 + 8, :]
                tiles.append(jnp.where(r8 + b0 < PEER_CNT[a], tile, NEG_INF))
                codes.append(a * k + b0 + r8)
        tiles.append(sv0[8:16, :] + sv1[0:1, :])
        codes.append((8 + r8) * k)
        pool = jnp.concatenate(tiles, axis=0)
        code = jnp.concatenate(codes, axis=0)
        ts = jnp.zeros((k, LANES), F32)
        ii = jnp.zeros((k, LANES), I32)
        jj = jnp.zeros((k, LANES), I32)
        ss = jnp.zeros((k, LANES), I32)
        for it in range(k):
            m = jnp.max(pool, axis=0, keepdims=True)
            sel = jnp.min(jnp.where(pool == m, code, k * k), axis=0, keepdims=True)
            pool = jnp.where(code == sel, NEG_INF, pool)
            a = sel >> 4
            b = sel & (k - 1)
            iv = jnp.sum(jnp.where(r16 == a, si0, 0), axis=0, keepdims=True)
            jv = jnp.sum(jnp.where(r16 == b, si1, 0), axis=0, keepdims=True)
            ts = jnp.where(r16 == it, m, ts)
            ii = jnp.where(r16 == it, iv, ii)
            jj = jnp.where(r16 == it, jv, jj)
            ss = jnp.where(r16 == it, _lookup16(a, PEER_OFF) + b, ss)
        e = jnp.exp(ts - jnp.max(ts, axis=0, keepdims=True))
        dst = pl.ds(pl.multiple_of(hh * k, k), k)
        g_ref[c, dst, :] = e / jnp.sum(e, axis=0, keepdims=True)
        i_ref[c, dst, :] = ii
        j_ref[c, dst, :] = jj
        s_ref[c, dst, :] = ss
        li_ref[c, dst, :] = si0
        lj_ref[c, dst, :] = si1
        return 0

    lax.fori_loop(0, nchunk * PEER_HEADS, stage2, 0)


def peer_route(x, shift, scale, wq, sk, *, per_row, rows_per_batch, tm):
    n, d = x.shape
    nchunk = tm // LANES
    npick = PEER_HEADS * PEER_TOPK
    nrow = sk.shape[0]
    pick = lambda dt: jax.ShapeDtypeStruct((n // LANES, npick, LANES), dt)
    pick_spec = pl.BlockSpec((nchunk, npick, LANES), lambda i: (i, 0, 0))
    outs = pl.pallas_call(
        _peer_route_kernel,
        out_shape=(jax.ShapeDtypeStruct((n, d), BF16), pick(F32), pick(I32), pick(I32), pick(I32), pick(I32), pick(I32)),
        grid=(n // tm,),
        in_specs=[pl.BlockSpec((tm, d), lambda i: (i, 0)),
                  _mod_specs(per_row, tm, d, rows_per_batch),
                  _mod_specs(per_row, tm, d, rows_per_batch),
                  pl.BlockSpec(wq.shape, lambda i: (0, 0)),
                  pl.BlockSpec(sk.shape, lambda i: (0, 0))],
        out_specs=(pl.BlockSpec((tm, d), lambda i: (i, 0)),) + (pick_spec,) * 6,
        scratch_shapes=[pltpu.VMEM((nchunk, nrow, LANES), F32),
                        pltpu.VMEM((nchunk, 2 * npick, LANES), F32),
                        pltpu.VMEM((nchunk, 2 * npick, LANES), I32)],
        compiler_params=_cparams(("arbitrary",)),
        name="peer_route",
    )(x, shift, scale, wq, sk)
    to_rows = lambda a: jnp.swapaxes(a, 1, 2).reshape(n, npick)
    return (outs[0],) + tuple(to_rows(a) for a in outs[1:])


def _peer_subkey_matrix(subkeys):
    h, p, nk, dk = subkeys.shape
    eye = jnp.eye(h * p, dtype=F32)
    m = eye[:, None, :, None] * subkeys.reshape(h * p, nk, 1, dk)
    return m.reshape(h * p * nk, h * p * dk).astype(BF16)


PEER_EB = 1024
PEER_RC = 32


def _gather_lanes(x, idx):
    return jnp.take_along_axis(x, idx, axis=1, mode="promise_in_bounds")


def _peer_up_kernel(hb_ref, u_ref, i_ref, j_ref, g_ref, c_ref, a_ref, acc_ref):
    s = pl.program_id(1)
    tm = hb_ref.shape[0]
    nslab = PEER_EB // LANES

    @pl.when(s == 0)
    def _():
        acc_ref[...] = jnp.zeros(acc_ref.shape, F32)

    a_ref[...] = _dot_nt(hb_ref[...], u_ref[...])

    def chunk(r, _):
        rows = pl.ds(pl.multiple_of(r * PEER_RC, PEER_RC), PEER_RC)
        ii = i_ref[rows, :]
        jj = j_ref[rows, :]
        acc = acc_ref[rows, :]
        for sl in range(nslab):
            got = _gather_lanes(a_ref[rows, sl * LANES:(sl + 1) * LANES], jj)
            acc = acc + jnp.where(ii == s * nslab + sl, got, 0.0)
        acc_ref[rows, :] = acc
        return 0

    lax.fori_loop(0, tm // PEER_RC, chunk, 0)

    @pl.when(s == pl.num_programs(1) - 1)
    def _():
        a = acc_ref[...]
        c_ref[...] = g_ref[...] * (0.5 * a * (1.0 + lax.erf(a * float(2.0 ** -0.5))))


def peer_up(hb, u, ii, jj, g, *, tm):
    n, d = hb.shape
    npick = ii.shape[1]
    pick_spec = pl.BlockSpec((tm, npick), lambda i, s: (i, 0))
    return pl.pallas_call(
        _peer_up_kernel,
        out_shape=jax.ShapeDtypeStruct((n, npick), F32),
        grid=(n // tm, u.shape[0] // PEER_EB),
        in_specs=[pl.BlockSpec((tm, d), lambda i, s: (i, 0)),
                  pl.BlockSpec((PEER_EB, d), lambda i, s: (s, 0)),
                  pick_spec, pick_spec, pick_spec],
        out_specs=pick_spec,
        scratch_shapes=[pltpu.VMEM((tm, PEER_EB), F32), pltpu.VMEM((tm, npick), F32)],
        compiler_params=_cparams(("arbitrary", "arbitrary")),
        name="peer_up",
    )(hb, u, ii, jj, g)


def _peer_down_kernel(x_ref, gate_ref, c_ref, s_ref, li_ref, lj_ref, v_ref, y_ref,
                      ctab_ref, invj_ref, mask_ref, oc_ref, wa_ref, acc_ref):
    s = pl.program_id(1)
    tm = x_ref.shape[0]
    k = PEER_TOPK
    nslab = PEER_EB // LANES

    @pl.when(s == 0)
    def _():
        acc_ref[...] = jnp.zeros(acc_ref.shape, F32)

        def build(r, _):
            rows = pl.ds(pl.multiple_of(r * PEER_RC, PEER_RC), PEER_RC)
            lane = lax.broadcasted_iota(I32, (PEER_RC, LANES), 1)
            cc = c_ref[rows, :]
            ss = s_ref[rows, :]
            li = li_ref[rows, :]
            lj = lj_ref[rows, :]
            for hh in range(PEER_HEADS):
                ctab = jnp.zeros((PEER_RC, LANES), F32)
                invj = jnp.full((PEER_RC, LANES), -1, I32)
                oc = jnp.zeros((PEER_RC, LANES), I32)
                for e in range(k):
                    col = hh * k + e
                    ctab = ctab + jnp.where(lane == ss[:, col:col + 1], cc[:, col:col + 1], 0.0)
                    invj = jnp.where(lane == lj[:, col:col + 1], e, invj)
                    oc = jnp.where(lane == li[:, col:col + 1], PEER_OFF[e] + (PEER_CNT[e] << PEER_OC_SHIFT), oc)
                ctab_ref[hh, rows, :] = ctab
                invj_ref[hh, rows, :] = jnp.maximum(invj, 0)
                mask_ref[hh, rows, :] = (invj >= 0).astype(F32)
                oc_ref[hh, rows, :] = oc
            return 0

        lax.fori_loop(0, tm // PEER_RC, build, 0)

    def chunk(r, _):
        rows = pl.ds(pl.multiple_of(r * PEER_RC, PEER_RC), PEER_RC)
        lane = lax.broadcasted_iota(I32, (PEER_RC, LANES), 1)
        slab_of_lane = s * nslab + (lane >> 4)
        b_of_lane = lane & (k - 1)
        was = [jnp.zeros((PEER_RC, LANES), F32) for _ in range(nslab)]
        for hh in range(PEER_HEADS):
            oc = _gather_lanes(oc_ref[hh, rows, :], slab_of_lane)
            cnt = oc >> PEER_OC_SHIFT
            off = oc & ((1 << PEER_OC_SHIFT) - 1)
            qc = _gather_lanes(ctab_ref[hh, rows, :], jnp.where(b_of_lane < cnt, off + b_of_lane, PEER_ZSLOT))
            invj = invj_ref[hh, rows, :]
            mask = mask_ref[hh, rows, :]
            for sl in range(nslab):
                was[sl] = was[sl] + _gather_lanes(qc, invj + sl * k) * mask
        for sl in range(nslab):
            wa_ref[rows, sl * LANES:(sl + 1) * LANES] = was[sl].astype(BF16)
        return 0

    lax.fori_loop(0, tm // PEER_RC, chunk, 0)
    acc_ref[...] += _dot(wa_ref[...], v_ref[...])

    @pl.when(s == pl.num_programs(1) - 1)
    def _():
        y_ref[...] = x_ref[...] + gate_ref[...] * acc_ref[...]


def peer_down(x, gate, c, slots, li, lj, v, *, per_row, rows_per_batch, tm):
    n, d = x.shape
    npick = c.shape[1]
    pick_spec = pl.BlockSpec((tm, npick), lambda i, s: (i, 0))
    gate_spec = (pl.BlockSpec((tm, d), lambda i, s: (i, 0)) if per_row else
                 pl.BlockSpec((None, 1, d), lambda i, s: ((i * tm) // rows_per_batch, 0, 0)))
    return pl.pallas_call(
        _peer_down_kernel,
        out_shape=jax.ShapeDtypeStruct((n, d), F32),
        grid=(n // tm, v.shape[0] // PEER_EB),
        in_specs=[pl.BlockSpec((tm, d), lambda i, s: (i, 0)), gate_spec,
                  pick_spec, pick_spec, pick_spec, pick_spec,
                  pl.BlockSpec((PEER_EB, d), lambda i, s: (s, 0))],
        out_specs=pl.BlockSpec((tm, d), lambda i, s: (i, 0)),
        scratch_shapes=[pltpu.VMEM((PEER_HEADS, tm, LANES), F32), pltpu.VMEM((PEER_HEADS, tm, LANES), I32),
                        pltpu.VMEM((PEER_HEADS, tm, LANES), F32), pltpu.VMEM((PEER_HEADS, tm, LANES), I32),
                        pltpu.VMEM((tm, PEER_EB), BF16), pltpu.VMEM((tm, d), F32)],
        compiler_params=_cparams(("arbitrary", "arbitrary")),
        name="peer_down",
    )(x, gate, c, slots, li, lj, v)


def peer_sublayer(x, shift, scale, gate, wq, sk, u, v, *, per_row, rows_per_batch, tm):
    hb, g, ii, jj, slots, li, lj = peer_route(x, shift, scale, wq, sk, per_row=per_row,
                                              rows_per_batch=rows_per_batch, tm=tm)
    c = peer_up(hb, u, ii, jj, g, tm=tm)
    return peer_down(x, gate, c, slots, li, lj, v, per_row=per_row, rows_per_batch=rows_per_batch, tm=tm)


TM_PROJ = 256
TM_PEER = 512


def kernel(x_prompt, x_sample, c_prompt, c_sample, cache_mla, cache_dil0, cache_dil1, cache_dil2, page_table, a_mod_w, a_mod_b, a_w_dq, a_g_cq, a_w_uq, a_w_dkv, a_g_ckv, a_g_qn, a_g_qr, a_g_kr, a_w_uk, a_g_kn, a_w_uv, a_w_o, kv_mod_w, kv_mod_b, kv_w, kv_g_k, b_mod_w, b_mod_b, b_w_q, b_g_q, b_w_o, f_mod_w, f_mod_b, f_w_q, f_subkeys, f_u, f_v):
    b, s, d = x_prompt.shape
    db, t, _ = x_sample.shape
    n_p, n_s = b * s, db * t
    past = page_table.shape[1] * PAGE_SIZE
    depth = f_mod_w.shape[0]
    n_a = a_mod_w.shape[0]
    caches = (cache_dil0, cache_dil1, cache_dil2)
    hw = DIL_HEADS * DIL_HEAD_DIM
    lat_w = cache_mla.shape[-1]

    c_all = jnp.concatenate([c_prompt, c_sample], axis=0)
    xp = x_prompt.reshape(n_p, d)
    xs = x_sample.reshape(n_s, d)
    pos_p = jnp.arange(s, dtype=I32)
    pos_s = jnp.tile(past + jnp.arange(t, dtype=I32), db)
    mla_tab_p = _rope_tables(pos_p, MLA_SLAB_ROPE0, MLA_ROPE // 2)
    mla_tab_s = _rope_tables(pos_s, MLA_SLAB_ROPE0, MLA_ROPE // 2)
    dil_tab_p = _rope_tables(pos_p, 0, DIL_ROT // 2)
    dil_tab_s = _rope_tables(pos_s, 0, DIL_ROT // 2)
    kw_p = dict(per_row=False, rows_per_batch=s)
    kw_s = dict(per_row=True, rows_per_batch=t)

    def mods(w, bias):
        m = modulation(c_all, w, bias)
        ms = jnp.repeat(m[b:], t, axis=0)
        k = w.shape[1] // d
        return ([m[:b, None, i * d:(i + 1) * d] for i in range(k)], [ms[:, i * d:(i + 1) * d] for i in range(k)])

    rows_p, rows_s = [], []
    dil_p, dil_s = [], []
    kvb_p = kv_new = None
    for layer in range(depth):
        if layer < n_a:
            i = layer
            (sh_p, sc_p, gt_p), (sh_s, sc_s, gt_s) = mods(a_mod_w[i], a_mod_b[i])
            w = _mla_weights(a_w_dq[i], a_g_cq[i], a_w_uq[i], a_w_dkv[i], a_g_ckv[i], a_g_qn[i], a_g_qr[i],
                             a_g_kr[i], a_w_uk[i], a_g_kn[i], a_w_uv[i])
            wo = a_w_o[i].astype(BF16)
            q, k, v, ckv, kpe = mla_project(xp, sh_p, sc_p, *mla_tab_p, w, tm=TM_PROJ, **kw_p)
            rows_p.append(jnp.concatenate([ckv, kpe[:, MLA_NOPE:MLA_NOPE + MLA_ROPE]], axis=1).reshape(b, s, lat_w))
            o = mla_attention(q, k, v, b, s)
            xp = out_proj(xp, o, wo, gt_p, tm=TM_PROJ, **kw_p)
            q, k, v, ckv, kpe = mla_project(xs, sh_s, sc_s, *mla_tab_s, w, tm=n_s, **kw_s)
            rows_s.append(jnp.concatenate([ckv, kpe[:, MLA_NOPE:MLA_NOPE + MLA_ROPE]], axis=1).reshape(db, t, lat_w))
            o = mla_sample_attention(q, ckv, kpe, cache_mla[i], page_table, a_w_uk[i], a_g_kn[i], a_w_uv[i], t)
            xs = out_proj(xs, o, wo, gt_s, tm=n_s, **kw_s)
        else:
            if layer == n_a:
                (sh_p, sc_p), (sh_s, sc_s) = mods(kv_mod_w, kv_mod_b)
                kvw = kv_w.reshape(d, 2, N_GROUPS, hw)
                wkv = jnp.stack([jnp.concatenate([kvw[:, 0, g], kvw[:, 1, g]], axis=1) for g in range(N_GROUPS)])
                wkv = wkv.astype(BF16)
                gk = kv_g_k[:, None, :]
                kvf_p, kvb_p = ada_heads(xp, sh_p, sc_p, *dil_tab_p, wkv, gk, n_norm=DIL_HEADS, oscale=1.0,
                                         emit_f32=True, tm=TM_PROJ, **kw_p)
                kvf_s, _ = ada_heads(xs, sh_s, sc_s, *dil_tab_s, wkv, gk, n_norm=DIL_HEADS, oscale=1.0,
                                     emit_f32=True, tm=n_s, **kw_s)
                new_p = kvf_p.reshape(b, s, N_GROUPS, 2, DIL_HEADS, DIL_HEAD_DIM)
                new_s = kvf_s.reshape(db, t, N_GROUPS, 2, DIL_HEADS, DIL_HEAD_DIM)
                for g in range(N_GROUPS):
                    win = DIL_WINDOWS[g]
                    full_s = jnp.concatenate([caches[g], new_s[:, :, g]], axis=1)
                    dil_p.append(new_p[:, s - min(win, s):, g])
                    dil_s.append(full_s[:, full_s.shape[1] - min(win, past + t):])
                kv_new = jnp.pad(kvf_s.reshape(db, t, -1), ((0, 0), (0, MLA_DEC_ROWS - t), (0, 0)))
            j = layer - n_a
            (sh_p, sc_p, gt_p), (sh_s, sc_s, gt_s) = mods(b_mod_w[j], b_mod_b[j])
            wq = jnp.transpose(b_w_q[j].reshape(d, N_GROUPS, hw), (1, 0, 2)).astype(BF16)
            gq = b_g_q[j][:, None, :]
            wo = b_w_o[j].astype(BF16)
            qscale = float(DIL_HEAD_DIM ** -0.5)
            (qb,) = ada_heads(xp, sh_p, sc_p, *dil_tab_p, wq, gq, n_norm=DIL_HEADS, oscale=qscale,
                              emit_f32=False, tm=TM_PROJ, **kw_p)
            parts = [dil_prompt_attention(qb, kvb_p, g, b, s) for g in range(N_GROUPS)]
            xp = dil_combine(xp, [p[0] for p in parts], [p[1] for p in parts], wo, gt_p, tm=TM_PROJ, **kw_p)
            (qb,) = ada_heads(xs, sh_s, sc_s, *dil_tab_s, wq, gq, n_norm=DIL_HEADS, oscale=qscale,
                              emit_f32=False, tm=n_s, **kw_s)
            parts = [dil_sample_attention(qb, caches[g], kv_new, g, t) for g in range(N_GROUPS)]
            xs = dil_combine(xs, [p[0] for p in parts], [p[1] for p in parts], wo, gt_s, tm=n_s, **kw_s)
        (sh_p, sc_p, gt_p), (sh_s, sc_s, gt_s) = mods(f_mod_w[layer], f_mod_b[layer])
        wq = f_w_q[layer].astype(BF16)
        sk = _peer_subkey_matrix(f_subkeys[layer])
        u = f_u[layer].astype(BF16)
        v = f_v[layer].astype(BF16)
        xp = peer_sublayer(xp, sh_p, sc_p, gt_p, wq, sk, u, v, tm=TM_PEER, **kw_p)
        xs = peer_sublayer(xs, sh_s, sc_s, gt_s, wq, sk, u, v, tm=n_s, **kw_s)
    return (xp.reshape(b, s, d), xs.reshape(db, t, d), jnp.stack(rows_p), jnp.stack(rows_s),
            dil_p[0], dil_s[0], dil_p[1], dil_s[1], dil_p[2], dil_s[2])
```

```python
import functools

import numpy as np
import jax
import jax.numpy as jnp
from jax import lax
from jax.experimental import pallas as pl
from jax.experimental.pallas import tpu as pltpu

F32 = jnp.float32
BF16 = jnp.bfloat16
I32 = jnp.int32

LANES = 128
VMEM_LIMIT = 56 * 1024 * 1024

ROPE_THETA = 500000.0
EPS = 1e-6
NEG = -1e30

MLA_HEADS = 16
MLA_NOPE = 64
MLA_ROPE = 32
MLA_V = 64
MLA_KV_LORA = 256
PAGE_SIZE = 128

DIL_WINDOWS = (128, 512, 2048)
DIL_RATES = (1, 4, 16)
N_GROUPS = 3
DIL_HEADS = 8
DIL_HEAD_DIM = 128
DIL_ROT = DIL_HEAD_DIM // 4

PEER_HEADS = 8
PEER_NKEYS = 128
PEER_TOPK = 16
PEER_DKEY = 128


def _cparams(sem):
    return pltpu.CompilerParams(dimension_semantics=sem, vmem_limit_bytes=VMEM_LIMIT)


def _rms(x):
    return x * lax.rsqrt(jnp.mean(x * x, axis=-1, keepdims=True) + EPS)


def _dot(a, b):
    return jnp.dot(a, b, preferred_element_type=F32)


def _dot_nt(a, b):
    return lax.dot_general(a, b, (((1,), (1,)), ((), ())), preferred_element_type=F32)


def _mod_specs(per_row, tm, d, rows_per_batch):
    if per_row:
        return pl.BlockSpec((tm, d), lambda i: (i, 0))
    return pl.BlockSpec((None, 1, d), lambda i: ((i * tm) // rows_per_batch, 0, 0))


def _mod_kernel(c_ref, w_ref, b_ref, o_ref):
    c = c_ref[...]
    s = (c * jax.nn.sigmoid(c)).astype(BF16)
    o_ref[...] = _dot(s, w_ref[...].astype(BF16)) + b_ref[...]


def modulation(c, w, b):
    bc, d = c.shape
    n = w.shape[1]
    tn = 1024
    return pl.pallas_call(
        _mod_kernel,
        out_shape=jax.ShapeDtypeStruct((bc, n), F32),
        grid=(n // tn,),
        in_specs=[pl.BlockSpec((bc, d), lambda j: (0, 0)),
                  pl.BlockSpec((d, tn), lambda j: (0, j)),
                  pl.BlockSpec((1, tn), lambda j: (0, j))],
        out_specs=pl.BlockSpec((bc, tn), lambda j: (0, j)),
        compiler_params=_cparams(("arbitrary",)),
        name="modulation",
    )(c, w, b.reshape(1, n))


def _rope_tables(pos, lane0, half):
    inv = ROPE_THETA ** (-jnp.arange(half, dtype=F32) / half)
    ang = pos.astype(F32)[:, None] * inv
    cos, sin = jnp.cos(ang), jnp.sin(ang)
    t = pos.shape[0]
    cos_t = jnp.ones((t, LANES), F32)
    sin_t = jnp.zeros((t, LANES), F32)
    cos_t = cos_t.at[:, lane0:lane0 + half].set(cos).at[:, lane0 + half:lane0 + 2 * half].set(cos)
    sin_t = sin_t.at[:, lane0:lane0 + half].set(-sin).at[:, lane0 + half:lane0 + 2 * half].set(sin)
    return cos_t, sin_t


def _rope_slab(s, cos_t, sin_t, lane, lane0, half):
    swapped = jnp.where(lane < lane0 + half, pltpu.roll(s, LANES - half, 1), pltpu.roll(s, half, 1))
    return s * cos_t + swapped * sin_t


MLA_SLAB_ROPE0 = MLA_NOPE


def _mla_proj_kernel(x_ref, sh_ref, sc_ref, cos_ref, sin_ref, wdq_ref, gcq_ref, wuq_ref, wdkv_ref, gckv_ref,
                     gq_ref, gkr_ref, wuk_ref, gkn_ref, wuv_ref,
                     q_ref, k_ref, v_ref, ckv_ref, kpe_ref, *, qscale):
    x = x_ref[...]
    tm = x.shape[0]
    h = _rms(x) * (1.0 + sc_ref[...]) + sh_ref[...]
    hb = h.astype(BF16)
    cq = _rms(_dot(hb, wdq_ref[...])) * gcq_ref[...]
    q = _dot(cq.astype(BF16), wuq_ref[...])
    kv = _dot(hb, wdkv_ref[...])
    ckv = _rms(kv[:, :MLA_KV_LORA]) * gckv_ref[...]
    cos_t, sin_t = cos_ref[...], sin_ref[...]
    lane = lax.broadcasted_iota(I32, (tm, LANES), 1)
    m_n = lane < MLA_NOPE
    m_r = (lane >= MLA_NOPE) & (lane < MLA_NOPE + MLA_ROPE)

    def norm_rope(s, g):
        s2 = s * s
        ssn = jnp.sum(jnp.where(m_n, s2, 0.0), axis=-1, keepdims=True)
        ssr = jnp.sum(jnp.where(m_r, s2, 0.0), axis=-1, keepdims=True)
        r = jnp.where(m_n, lax.rsqrt(ssn / MLA_NOPE + EPS), lax.rsqrt(ssr / MLA_ROPE + EPS))
        return _rope_slab(s * r * g, cos_t, sin_t, lane, MLA_SLAB_ROPE0, MLA_ROPE // 2)

    kpe = norm_rope(kv[:, MLA_KV_LORA:], gkr_ref[...])
    ckv_ref[...] = ckv
    kpe_ref[...] = kpe
    ckvb = ckv.astype(BF16)
    kn = _dot(ckvb, wuk_ref[...])
    v_ref[...] = _dot(ckvb, wuv_ref[...]).astype(BF16)
    gq = gq_ref[...]
    gkn = gkn_ref[...]
    for hh in range(MLA_HEADS):
        sl = slice(hh * LANES, (hh + 1) * LANES)
        q_ref[hh] = (norm_rope(q[:, sl], gq) * qscale).astype(BF16)
        ks = kn[:, sl]
        ssn = jnp.sum(ks * ks, axis=-1, keepdims=True)
        k_ref[hh] = (ks * lax.rsqrt(ssn / MLA_NOPE + EPS) * gkn + kpe).astype(BF16)


def _mla_weights(w_dq, g_cq, w_uq, w_dkv, g_ckv, g_qn, g_qr, g_kr, w_uk, g_kn, w_uv):
    d = w_dq.shape[0]
    hd = MLA_NOPE + MLA_ROPE
    pad = LANES - hd
    wuq = jnp.pad(w_uq.reshape(-1, MLA_HEADS, hd), ((0, 0), (0, 0), (0, pad))).reshape(-1, MLA_HEADS * LANES)
    wdkv = jnp.concatenate([w_dkv[:, :MLA_KV_LORA], jnp.zeros((d, MLA_NOPE), F32), w_dkv[:, MLA_KV_LORA:],
                            jnp.zeros((d, pad), F32)], axis=1)
    gq = jnp.concatenate([g_qn, g_qr, jnp.zeros((pad,), F32)])[None]
    gkr = jnp.concatenate([jnp.zeros((MLA_NOPE,), F32), g_kr, jnp.zeros((pad,), F32)])[None]
    wuk = jnp.pad(w_uk, ((0, 0), (0, 0), (0, LANES - MLA_NOPE))).reshape(MLA_KV_LORA, MLA_HEADS * LANES)
    gkn = jnp.concatenate([g_kn, jnp.zeros((LANES - MLA_NOPE,), F32)])[None]
    wuv = w_uv.reshape(MLA_KV_LORA, MLA_HEADS * MLA_V)
    return dict(wdq=w_dq.astype(BF16), gcq=g_cq[None], wuq=wuq.astype(BF16), wdkv=wdkv.astype(BF16),
                gckv=g_ckv[None], gq=gq, gkr=gkr, wuk=wuk.astype(BF16), gkn=gkn, wuv=wuv.astype(BF16))


def mla_project(x, shift, scale, cos_t, sin_t, w, *, per_row, rows_per_batch, tm):
    n, d = x.shape
    nblk_pos = cos_t.shape[0] // tm
    full = lambda a: pl.BlockSpec(a.shape, lambda i: (0,) * a.ndim)
    wnames = ("wdq", "gcq", "wuq", "wdkv", "gckv", "gq", "gkr", "wuk", "gkn", "wuv")
    qscale = float((MLA_NOPE + MLA_ROPE) ** -0.5)
    return pl.pallas_call(
        functools.partial(_mla_proj_kernel, qscale=qscale),
        out_shape=(jax.ShapeDtypeStruct((MLA_HEADS, n, LANES), BF16),
                   jax.ShapeDtypeStruct((MLA_HEADS, n, LANES), BF16),
                   jax.ShapeDtypeStruct((n, MLA_HEADS * MLA_V), BF16),
                   jax.ShapeDtypeStruct((n, MLA_KV_LORA), F32),
                   jax.ShapeDtypeStruct((n, LANES), F32)),
        grid=(n // tm,),
        in_specs=[pl.BlockSpec((tm, d), lambda i: (i, 0)),
                  _mod_specs(per_row, tm, d, rows_per_batch),
                  _mod_specs(per_row, tm, d, rows_per_batch),
                  pl.BlockSpec((tm, LANES), lambda i: (i % nblk_pos, 0)),
                  pl.BlockSpec((tm, LANES), lambda i: (i % nblk_pos, 0))] + [full(w[k]) for k in wnames],
        out_specs=(pl.BlockSpec((MLA_HEADS, tm, LANES), lambda i: (0, i, 0)),
                   pl.BlockSpec((MLA_HEADS, tm, LANES), lambda i: (0, i, 0)),
                   pl.BlockSpec((tm, MLA_HEADS * MLA_V), lambda i: (i, 0)),
                   pl.BlockSpec((tm, MLA_KV_LORA), lambda i: (i, 0)),
                   pl.BlockSpec((tm, LANES), lambda i: (i, 0))),
        compiler_params=_cparams(("arbitrary",)),
        name="mla_project",
    )(x, shift, scale, cos_t, sin_t, *[w[k] for k in wnames])


def _mla_attn_kernel(q_ref, k_ref, v_ref, o_ref, *, tq):
    qi = pl.program_id(2)
    row = lax.broadcasted_iota(I32, (tq, tq), 0)
    col = lax.broadcasted_iota(I32, (tq, tq), 1)
    outs = []
    for hh in range(2):
        q = q_ref[hh]

        def body(j, carry, hh=hh, q=q):
            m, l, acc = carry
            off = pl.multiple_of(j * tq, tq)
            kc = k_ref[hh, pl.ds(off, tq), :]
            vc = v_ref[pl.ds(off, tq), :]
            s = _dot_nt(q, kc)
            s = jnp.where(col + (j - qi) * tq <= row, s, NEG)
            m_new = jnp.maximum(m, jnp.max(s, axis=-1, keepdims=True))
            alpha = jnp.exp(m - m_new)
            p = jnp.exp(s - m_new)
            l = alpha * l + jnp.sum(p, axis=-1, keepdims=True)
            acc = alpha * acc + _dot(p.astype(BF16), vc)
            return m_new, l, acc

        m0 = jnp.full((tq, 1), NEG, F32)
        l0 = jnp.zeros((tq, 1), F32)
        a0 = jnp.zeros((tq, LANES), F32)
        m, l, acc = lax.fori_loop(0, qi + 1, body, (m0, l0, a0))
        outs.append(acc / l)
    lane = lax.broadcasted_iota(I32, (tq, LANES), 1)
    o_ref[...] = jnp.where(lane < MLA_V, outs[0], outs[1]).astype(BF16)


def mla_attention(q, k, v, b, s):
    tq = 256
    hp = MLA_HEADS // 2
    q4 = q.reshape(MLA_HEADS, b, s, LANES)
    k4 = k.reshape(MLA_HEADS, b, s, LANES)
    v3 = v.reshape(b, s, MLA_HEADS * MLA_V)
    out = pl.pallas_call(
        functools.partial(_mla_attn_kernel, tq=tq),
        out_shape=jax.ShapeDtypeStruct((b, s, MLA_HEADS * MLA_V), BF16),
        grid=(b, hp, s // tq),
        in_specs=[pl.BlockSpec((2, None, tq, LANES), lambda bi, h, qi: (h, bi, qi, 0)),
                  pl.BlockSpec((2, None, s, LANES), lambda bi, h, qi: (h, bi, 0, 0)),
                  pl.BlockSpec((None, s, LANES), lambda bi, h, qi: (bi, 0, h))],
        out_specs=pl.BlockSpec((None, tq, LANES), lambda bi, h, qi: (bi, qi, h)),
        compiler_params=_cparams(("arbitrary", "arbitrary", "arbitrary")),
        name="mla_attention",
    )(q4, k4, v3)
    return out.reshape(b * s, MLA_HEADS * MLA_V)


def _out_proj_kernel(x_ref, o_ref, w_ref, g_ref, y_ref):
    y_ref[...] = x_ref[...] + g_ref[...] * _dot(o_ref[...], w_ref[...])


def out_proj(x, o, w, gate, *, per_row, rows_per_batch, tm):
    n, d = x.shape
    kdim = o.shape[1]
    return pl.pallas_call(
        _out_proj_kernel,
        out_shape=jax.ShapeDtypeStruct((n, d), F32),
        grid=(n // tm,),
        in_specs=[pl.BlockSpec((tm, d), lambda i: (i, 0)),
                  pl.BlockSpec((tm, kdim), lambda i: (i, 0)),
                  pl.BlockSpec((kdim, d), lambda i: (0, 0)),
                  _mod_specs(per_row, tm, d, rows_per_batch)],
        out_specs=pl.BlockSpec((tm, d), lambda i: (i, 0)),
        compiler_params=_cparams(("arbitrary",)),
        name="out_proj",
    )(x, o, w, gate)


MLA_DEC_PAGES = 4
MLA_DEC_ROWS = 8


def _qabs_kernel(q_ref, w_ref, o_ref):
    o_ref[...] = _dot(q_ref[...], w_ref[...]).astype(BF16)


def mla_absorb_q(q, w_uk, g_kn):
    h, n, _ = q.shape
    wt = jnp.transpose(w_uk, (1, 2, 0)) * g_kn[None, :, None]
    wt = jnp.pad(wt, ((0, 0), (0, LANES - MLA_NOPE), (0, 0))).astype(BF16)
    return pl.pallas_call(
        _qabs_kernel,
        out_shape=jax.ShapeDtypeStruct((h, n, MLA_KV_LORA), BF16),
        grid=(h,),
        in_specs=[pl.BlockSpec((None, n, LANES), lambda i: (i, 0, 0)),
                  pl.BlockSpec((None, LANES, MLA_KV_LORA), lambda i: (i, 0, 0))],
        out_specs=pl.BlockSpec((None, n, MLA_KV_LORA), lambda i: (i, 0, 0)),
        compiler_params=_cparams(("arbitrary",)),
        name="mla_absorb_q",
    )(q, wt)


def _mla_decode_kernel(pt_ref, qa_ref, qp_ref, new_ref, wuk_ref, ind_ref, *rest, n_dec):
    page_refs = rest[:MLA_DEC_PAGES]
    o_ref = rest[MLA_DEC_PAGES]
    m_ref, l_ref, acc_ref = rest[MLA_DEC_PAGES + 1:]
    j = pl.program_id(1)
    qa = qa_ref[...]
    qp = qp_ref[...]

    def scores(lat):
        ckvb = lat[:, :MLA_KV_LORA].astype(BF16)
        kpeb = lat[:, MLA_KV_LORA:].astype(BF16)
        kn = _dot(ckvb, wuk_ref[...])
        ssq = _dot((kn * kn).astype(BF16), ind_ref[...])
        r = lax.rsqrt(ssq / MLA_NOPE + EPS)
        return _dot(ckvb, qa) * r + _dot(kpeb, qp), ckvb

    def accumulate(s, ckvb):
        m = m_ref[...]
        m_new = jnp.maximum(m, jnp.max(s, axis=0, keepdims=True))
        alpha = jnp.exp(m - m_new)
        p = jnp.exp(s - m_new)
        l_ref[...] = alpha * l_ref[...] + jnp.sum(p, axis=0, keepdims=True)
        pv = lax.dot_general(ckvb, p.astype(BF16), (((0,), (0,)), ((), ())), preferred_element_type=F32)
        acc_ref[...] = alpha * acc_ref[...] + pv
        m_ref[...] = m_new

    @pl.when(j == 0)
    def _():
        m_ref[...] = jnp.full(m_ref.shape, NEG, F32)
        l_ref[...] = jnp.zeros(l_ref.shape, F32)
        acc_ref[...] = jnp.zeros(acc_ref.shape, F32)
        s, ckvb = scores(new_ref[...])
        key = lax.broadcasted_iota(I32, s.shape, 0)
        tok = lax.broadcasted_iota(I32, s.shape, 1) % n_dec
        accumulate(jnp.where((key <= tok) & (key < n_dec), s, NEG), ckvb)

    lat = jnp.concatenate([r[...] for r in page_refs], axis=0)
    s, ckvb = scores(lat)
    accumulate(s, ckvb)

    @pl.when(j == pl.num_programs(1) - 1)
    def _():
        o_ref[...] = acc_ref[...] / l_ref[...]


def mla_decode_attention(qabs_t, qpe_t, lat_new, pool, page_table, w_uk, n_dec):
    db, n_pages = page_table.shape
    lat = pool.shape[-1]
    wuk = w_uk.reshape(MLA_KV_LORA, MLA_HEADS * MLA_NOPE).astype(BF16)
    col = np.arange(LANES)
    ind = (np.arange(MLA_HEADS * MLA_NOPE)[:, None] // MLA_NOPE == (col // n_dec)[None, :]) & (col[None, :] < MLA_HEADS * n_dec)
    ind = jnp.asarray(ind, BF16)
    steps = n_pages // MLA_DEC_PAGES
    page_specs = [pl.BlockSpec((None, PAGE_SIZE, lat),
                               lambda b, j, pt, pp=pp: (pt[b * n_pages + j * MLA_DEC_PAGES + pp], 0, 0))
                  for pp in range(MLA_DEC_PAGES)]
    grid_spec = pltpu.PrefetchScalarGridSpec(
        num_scalar_prefetch=1,
        grid=(db, steps),
        in_specs=[pl.BlockSpec((None, MLA_KV_LORA, LANES), lambda b, j, pt: (b, 0, 0)),
                  pl.BlockSpec((None, MLA_ROPE, LANES), lambda b, j, pt: (b, 0, 0)),
                  pl.BlockSpec((None, MLA_DEC_ROWS, lat), lambda b, j, pt: (b, 0, 0)),
                  pl.BlockSpec(wuk.shape, lambda b, j, pt: (0, 0)),
                  pl.BlockSpec(ind.shape, lambda b, j, pt: (0, 0))] + page_specs,
        out_specs=pl.BlockSpec((None, MLA_KV_LORA, LANES), lambda b, j, pt: (b, 0, 0)),
        scratch_shapes=[pltpu.VMEM((1, LANES), F32), pltpu.VMEM((1, LANES), F32),
                        pltpu.VMEM((MLA_KV_LORA, LANES), F32)],
    )
    return pl.pallas_call(
        functools.partial(_mla_decode_kernel, n_dec=n_dec),
        out_shape=jax.ShapeDtypeStruct((db, MLA_KV_LORA, LANES), F32),
        grid_spec=grid_spec,
        compiler_params=_cparams(("arbitrary", "arbitrary")),
        name="mla_decode_attention",
    )(page_table.reshape(-1), qabs_t, qpe_t, lat_new, wuk, ind, *([pool] * MLA_DEC_PAGES))


def _mla_uv_kernel(x_ref, w_ref, o_ref):
    w = w_ref[...]
    lane = lax.broadcasted_iota(I32, o_ref.shape, 1)
    o_ref[...] = jnp.where(lane < MLA_V, _dot(x_ref[0], w), _dot(x_ref[1], w)).astype(BF16)


def mla_decode_values(x, w_uv):
    h, n, _ = x.shape
    wuv = w_uv.reshape(MLA_KV_LORA, MLA_HEADS * MLA_V).astype(BF16)
    return pl.pallas_call(
        _mla_uv_kernel,
        out_shape=jax.ShapeDtypeStruct((n, MLA_HEADS * MLA_V), BF16),
        grid=(h // 2,),
        in_specs=[pl.BlockSpec((2, n, MLA_KV_LORA), lambda i: (i, 0, 0)),
                  pl.BlockSpec((MLA_KV_LORA, LANES), lambda i: (0, i))],
        out_specs=pl.BlockSpec((n, LANES), lambda i: (0, i)),
        compiler_params=_cparams(("arbitrary",)),
        name="mla_decode_values",
    )(x, wuv)


def mla_sample_attention(q, ckv, kpe, pool, page_table, w_uk, g_kn, w_uv, n_dec):
    db = page_table.shape[0]
    n = db * n_dec
    qabs = mla_absorb_q(q, w_uk, g_kn)
    ncol = MLA_HEADS * n_dec

    def to_cols(a):
        f = a.shape[-1]
        a = a.reshape(MLA_HEADS, db, n_dec, f).transpose(1, 3, 0, 2).reshape(db, f, ncol)
        return jnp.pad(a, ((0, 0), (0, 0), (0, LANES - ncol)))

    qabs_t = to_cols(qabs)
    qpe_t = to_cols(q[:, :, MLA_NOPE:MLA_NOPE + MLA_ROPE])
    lat_new = jnp.concatenate([ckv, kpe[:, MLA_NOPE:MLA_NOPE + MLA_ROPE]], axis=1).reshape(db, n_dec, -1)
    lat_new = jnp.pad(lat_new, ((0, 0), (0, MLA_DEC_ROWS - n_dec), (0, 0)))
    out_t = mla_decode_attention(qabs_t, qpe_t, lat_new, pool, page_table, w_uk, n_dec)
    x = out_t[:, :, :ncol].reshape(db, MLA_KV_LORA, MLA_HEADS, n_dec).transpose(2, 0, 3, 1)
    x = x.reshape(MLA_HEADS, n, MLA_KV_LORA).astype(BF16)
    return mla_decode_values(x, w_uv)


def _ada_heads_kernel(x_ref, sh_ref, sc_ref, cos_ref, sin_ref, w_ref, g_ref, *out_refs, n_norm, n_slab, oscale, emit_f32):
    x = x_ref[...]
    tm = x.shape[0]
    h = _rms(x) * (1.0 + sc_ref[...]) + sh_ref[...]
    y = _dot(h.astype(BF16), w_ref[...])
    cos_t, sin_t = cos_ref[...], sin_ref[...]
    g = g_ref[...]
    lane = lax.broadcasted_iota(I32, (tm, LANES), 1)
    ob_ref = out_refs[-1]
    for s in range(n_slab):
        sl = slice(s * LANES, (s + 1) * LANES)
        ys = y[:, sl]
        if s < n_norm:
            ys = _rope_slab(_rms(ys) * g, cos_t, sin_t, lane, 0, DIL_ROT // 2)
        if emit_f32:
            out_refs[0][:, sl] = ys
        ob_ref[:, sl] = (ys * oscale).astype(BF16) if s < n_norm else ys.astype(BF16)


def ada_heads(x, shift, scale, cos_t, sin_t, w, g, *, n_norm, oscale, emit_f32, per_row, rows_per_batch, tm):
    n, d = x.shape
    ng, _, gw = w.shape
    n_slab = gw // LANES
    nblk_pos = cos_t.shape[0] // tm
    mod_spec = (pl.BlockSpec((tm, d), lambda gi, i: (i, 0)) if per_row else
                pl.BlockSpec((None, 1, d), lambda gi, i: ((i * tm) // rows_per_batch, 0, 0)))
    out_shape = [jax.ShapeDtypeStruct((n, ng * gw), BF16)]
    out_specs = [pl.BlockSpec((tm, gw), lambda gi, i: (i, gi))]
    if emit_f32:
        out_shape = [jax.ShapeDtypeStruct((n, ng * gw), F32)] + out_shape
        out_specs = [pl.BlockSpec((tm, gw), lambda gi, i: (i, gi))] + out_specs
    return pl.pallas_call(
        functools.partial(_ada_heads_kernel, n_norm=n_norm, n_slab=n_slab, oscale=oscale, emit_f32=emit_f32),
        out_shape=tuple(out_shape),
        grid=(ng, n // tm),
        in_specs=[pl.BlockSpec((tm, d), lambda gi, i: (i, 0)), mod_spec, mod_spec,
                  pl.BlockSpec((tm, LANES), lambda gi, i: (i % nblk_pos, 0)),
                  pl.BlockSpec((tm, LANES), lambda gi, i: (i % nblk_pos, 0)),
                  pl.BlockSpec((None, d, gw), lambda gi, i: (gi, 0, 0)),
                  pl.BlockSpec((None, 1, LANES), lambda gi, i: (gi, 0, 0))],
        out_specs=tuple(out_specs),
        compiler_params=_cparams(("arbitrary", "arbitrary")),
        name="ada_heads",
    )(x, shift, scale, cos_t, sin_t, w, g)


DIL_SPAN = 128
STAT_DEN0 = DIL_HEADS


def _dil_prompt_kernel(q_ref, kp_ref, vp_ref, kc_ref, vc_ref, num_ref, st_ref):
    nb = pl.program_id(2)
    span = DIL_SPAN
    iq = lax.broadcasted_iota(I32, (span, span), 0)
    jk = lax.broadcasted_iota(I32, (span, span), 1)
    mask_prev = (jk >= iq) & (nb > 0)
    mask_cur = jk <= iq
    lane = lax.broadcasted_iota(I32, (span, LANES), 1)
    stats = jnp.zeros((span, LANES), F32)
    for hh in range(DIL_HEADS):
        sl = slice(hh * LANES, (hh + 1) * LANES)
        q = q_ref[:, sl]
        sp = jnp.where(mask_prev, _dot_nt(q, kp_ref[:, sl]), NEG)
        sc = jnp.where(mask_cur, _dot_nt(q, kc_ref[:, sl]), NEG)
        m = jnp.maximum(jnp.max(sp, axis=-1, keepdims=True), jnp.max(sc, axis=-1, keepdims=True))
        pp = jnp.exp(sp - m)
        pc = jnp.exp(sc - m)
        den = jnp.sum(pp, axis=-1, keepdims=True) + jnp.sum(pc, axis=-1, keepdims=True)
        num_ref[:, sl] = _dot(pp.astype(BF16), vp_ref[:, sl]) + _dot(pc.astype(BF16), vc_ref[:, sl])
        stats = jnp.where(lane == hh, m, stats)
        stats = jnp.where(lane == STAT_DEN0 + hh, den, stats)
    st_ref[...] = stats


def dil_prompt_attention(q, kv, g, b, s):
    dil = DIL_RATES[g]
    span = DIL_WINDOWS[g] // dil
    assert span == DIL_SPAN and s % (dil * span) == 0
    n_sub = s // dil
    n_blk = n_sub // span
    hw = DIL_HEADS * DIL_HEAD_DIM
    qw, kw = q.shape[1], kv.shape[1]
    qv = q.reshape(b, n_sub, dil * qw)
    kvv = kv.reshape(b, n_sub, dil * kw)
    qb, kb = qw // hw, kw // hw
    prev = lambda nb: jnp.maximum(nb - 1, 0)
    num, st = pl.pallas_call(
        _dil_prompt_kernel,
        out_shape=(jax.ShapeDtypeStruct((b, n_sub, dil * hw), F32),
                   jax.ShapeDtypeStruct((b, n_sub, dil * LANES), F32)),
        grid=(b, dil, n_blk),
        in_specs=[pl.BlockSpec((None, span, hw), lambda bi, r, nb: (bi, nb, r * qb + g)),
                  pl.BlockSpec((None, span, hw), lambda bi, r, nb: (bi, prev(nb), r * kb + 2 * g)),
                  pl.BlockSpec((None, span, hw), lambda bi, r, nb: (bi, prev(nb), r * kb + 2 * g + 1)),
                  pl.BlockSpec((None, span, hw), lambda bi, r, nb: (bi, nb, r * kb + 2 * g)),
                  pl.BlockSpec((None, span, hw), lambda bi, r, nb: (bi, nb, r * kb + 2 * g + 1))],
        out_specs=(pl.BlockSpec((None, span, hw), lambda bi, r, nb: (bi, nb, r)),
                   pl.BlockSpec((None, span, LANES), lambda bi, r, nb: (bi, nb, r))),
        compiler_params=_cparams(("arbitrary", "arbitrary", "arbitrary")),
        name=f"dil_prompt_attention_g{g}",
    )(qv, kvv, kvv, kvv, kvv)
    return num.reshape(b * s, hw), st.reshape(b * s, LANES)


def _dil_sample_kernel(q_ref, cache_ref, new_ref, num_ref, st_ref, *, dil, n_dec):
    t = pl.program_id(1)
    hw = DIL_HEADS * DIL_HEAD_DIM
    rows = cache_ref.shape[0]
    n_new = new_ref.shape[0]
    crow = lax.broadcasted_iota(I32, (rows, 1), 0)
    nrow = lax.broadcasted_iota(I32, (n_new, 1), 0)
    cache_ok = (crow >= t) if dil == 1 else (crow >= 0)
    new_ok = (nrow <= t) & (nrow < n_dec) if dil == 1 else (nrow == t)
    lane = lax.broadcasted_iota(I32, (1, LANES), 1)
    stats = jnp.zeros((1, LANES), F32)
    for hh in range(DIL_HEADS):
        sl = slice(hh * LANES, (hh + 1) * LANES)
        q = q_ref[:, sl].astype(F32)
        sc = jnp.where(cache_ok, jnp.sum(cache_ref[:, sl] * q, axis=-1, keepdims=True), NEG)
        sn = jnp.where(new_ok, jnp.sum(new_ref[:, sl] * q, axis=-1, keepdims=True), NEG)
        m = jnp.maximum(jnp.max(sc, axis=0, keepdims=True), jnp.max(sn, axis=0, keepdims=True))
        pc = jnp.exp(sc - m)
        pn = jnp.exp(sn - m)
        den = jnp.sum(pc, axis=0, keepdims=True) + jnp.sum(pn, axis=0, keepdims=True)
        vs = slice(hw + hh * LANES, hw + (hh + 1) * LANES)
        num_ref[:, sl] = (jnp.sum(pc * cache_ref[:, vs], axis=0, keepdims=True)
                          + jnp.sum(pn * new_ref[:, vs], axis=0, keepdims=True))
        stats = jnp.where(lane == hh, m, stats)
        stats = jnp.where(lane == STAT_DEN0 + hh, den, stats)
    st_ref[...] = stats


def dil_sample_attention(q, cache, kv_new, g, n_dec):
    dil = DIL_RATES[g]
    db, w = cache.shape[:2]
    assert w == DIL_WINDOWS[g] and w // dil == DIL_SPAN and (dil == 1 or n_dec <= dil)
    hw = DIL_HEADS * DIL_HEAD_DIM
    rows = w // dil
    cv = cache.reshape(db, rows, dil * 2 * hw)
    q4 = q.reshape(db, n_dec, 1, q.shape[1])
    n_new = kv_new.shape[1]
    num, st = pl.pallas_call(
        functools.partial(_dil_sample_kernel, dil=dil, n_dec=n_dec),
        out_shape=(jax.ShapeDtypeStruct((db, n_dec, 1, hw), F32),
                   jax.ShapeDtypeStruct((db, n_dec, 1, LANES), F32)),
        grid=(db, n_dec),
        in_specs=[pl.BlockSpec((None, None, 1, hw), lambda bi, t: (bi, t, 0, g)),
                  pl.BlockSpec((None, rows, 2 * hw), lambda bi, t: (bi, 0, t if dil > 1 else 0)),
                  pl.BlockSpec((None, n_new, 2 * hw), lambda bi, t: (bi, 0, g))],
        out_specs=(pl.BlockSpec((None, None, 1, hw), lambda bi, t: (bi, t, 0, 0)),
                   pl.BlockSpec((None, None, 1, LANES), lambda bi, t: (bi, t, 0, 0))),
        compiler_params=_cparams(("arbitrary", "arbitrary")),
        name=f"dil_sample_attention_g{g}",
    )(q4, cv, kv_new)
    return num.reshape(db * n_dec, hw), st.reshape(db * n_dec, LANES)


def _dil_combine_kernel(x_ref, n0_ref, n1_ref, n2_ref, s0_ref, s1_ref, s2_ref, w_ref, g_ref, y_ref):
    nums = (n0_ref, n1_ref, n2_ref)
    stats = (s0_ref[...], s1_ref[...], s2_ref[...])
    outs = []
    for hh in range(DIL_HEADS):
        sl = slice(hh * LANES, (hh + 1) * LANES)
        ms = [st[:, hh:hh + 1] for st in stats]
        ds = [st[:, STAT_DEN0 + hh:STAT_DEN0 + hh + 1] for st in stats]
        big = jnp.maximum(jnp.maximum(ms[0], ms[1]), ms[2])
        num = 0.0
        den = 0.0
        for gi in range(N_GROUPS):
            wgt = jnp.exp(ms[gi] - big)
            num = num + wgt * nums[gi][:, sl]
            den = den + wgt * ds[gi]
        outs.append((num / den).astype(BF16))
    o = jnp.concatenate(outs, axis=1)
    y_ref[...] = x_ref[...] + g_ref[...] * _dot(o, w_ref[...])


def dil_combine(x, nums, stats, w, gate, *, per_row, rows_per_batch, tm):
    n, d = x.shape
    hw = DIL_HEADS * DIL_HEAD_DIM
    row = lambda width: pl.BlockSpec((tm, width), lambda i: (i, 0))
    return pl.pallas_call(
        _dil_combine_kernel,
        out_shape=jax.ShapeDtypeStruct((n, d), F32),
        grid=(n // tm,),
        in_specs=[row(d)] + [row(hw)] * 3 + [row(LANES)] * 3 + [pl.BlockSpec((hw, d), lambda i: (0, 0)),
                                                                 _mod_specs(per_row, tm, d, rows_per_batch)],
        out_specs=row(d),
        compiler_params=_cparams(("arbitrary",)),
        name="dil_combine",
    )(x, *nums, *stats, w, gate)


PEER_CNT = tuple(PEER_TOPK // (a + 1) for a in range(PEER_TOPK))
PEER_OFF = tuple(int(v) for v in np.cumsum((0,) + PEER_CNT[:-1]))
PEER_ZSLOT = LANES - 1
PEER_OC_SHIFT = 6
NEG_INF = float("-inf")


def _lookup16(idx, table):
    out = jnp.zeros_like(idx)
    for a, v in enumerate(table):
        if v:
            out = jnp.where(idx == a, v, out)
    return out


def _peer_route_kernel(x_ref, sh_ref, sc_ref, wq_ref, sk_ref,
                       hb_ref, g_ref, i_ref, j_ref, s_ref, li_ref, lj_ref,
                       st_ref, sv_ref, si_ref):
    k = PEER_TOPK
    nk = PEER_NKEYS
    x = x_ref[...]
    tm = x.shape[0]
    nchunk = tm // LANES
    h = _rms(x) * (1.0 + sc_ref[...]) + sh_ref[...]
    hb = h.astype(BF16)
    hb_ref[...] = hb
    q = _dot(hb, wq_ref[...]).astype(BF16)
    st = _dot_nt(sk_ref[...], q)
    for c in range(nchunk):
        st_ref[c] = st[:, c * LANES:(c + 1) * LANES]
    rk = lax.broadcasted_iota(I32, (nk, LANES), 0)
    r16 = lax.broadcasted_iota(I32, (k, LANES), 0)
    r8 = lax.broadcasted_iota(I32, (8, LANES), 0)

    def stage1(t, _):
        c = t // (2 * PEER_HEADS)
        hp = t % (2 * PEER_HEADS)
        s = st_ref[c, pl.ds(pl.multiple_of(hp * nk, nk), nk), :]
        sv = jnp.zeros((k, LANES), F32)
        si = jnp.zeros((k, LANES), I32)
        for it in range(k):
            m = jnp.max(s, axis=0, keepdims=True)
            idx = jnp.min(jnp.where(s == m, rk, nk), axis=0, keepdims=True)
            s = jnp.where(rk == idx, NEG_INF, s)
            sv = jnp.where(r16 == it, m, sv)
            si = jnp.where(r16 == it, idx, si)
        sv_ref[c, pl.ds(pl.multiple_of(hp * k, k), k), :] = sv
        si_ref[c, pl.ds(pl.multiple_of(hp * k, k), k), :] = si
        return 0

    lax.fori_loop(0, nchunk * 2 * PEER_HEADS, stage1, 0)

    def stage2(t, _):
        c = t // PEER_HEADS
        hh = t % PEER_HEADS
        o0 = pl.multiple_of(hh * 2 * k, 2 * k)
        o1 = pl.multiple_of(hh * 2 * k + k, k)
        sv0 = sv_ref[c, pl.ds(o0, k), :]
        sv1 = sv_ref[c, pl.ds(o1, k), :]
        si0 = si_ref[c, pl.ds(o0, k), :]
        si1 = si_ref[c, pl.ds(o1, k), :]
        tiles, codes = [], []
        for a in range(8):
            for b0 in range(0, PEER_CNT[a], 8):
                tile = sv0[a:a + 1, :] + sv1[b0:b0 + 8, :]
                tiles.append(jnp.where(r8 + b0 < PEER_CNT[a], tile, NEG_INF))
                codes.append(a * k + b0 + r8)
        tiles.append(sv0[8:16, :] + sv1[0:1, :])
        codes.append((8 + r8) * k)
        pool = jnp.concatenate(tiles, axis=0)
        code = jnp.concatenate(codes, axis=0)
        ts = jnp.zeros((k, LANES), F32)
        ii = jnp.zeros((k, LANES), I32)
        jj = jnp.zeros((k, LANES), I32)
        ss = jnp.zeros((k, LANES), I32)
        for it in range(k):
            m = jnp.max(pool, axis=0, keepdims=True)
            sel = jnp.min(jnp.where(pool == m, code, k * k), axis=0, keepdims=True)
            pool = jnp.where(code == sel, NEG_INF, pool)
            a = sel >> 4
            b = sel & (k - 1)
            iv = jnp.sum(jnp.where(r16 == a, si0, 0), axis=0, keepdims=True)
            jv = jnp.sum(jnp.where(r16 == b, si1, 0), axis=0, keepdims=True)
            ts = jnp.where(r16 == it, m, ts)
            ii = jnp.where(r16 == it, iv, ii)
            jj = jnp.where(r16 == it, jv, jj)
            ss = jnp.where(r16 == it, _lookup16(a, PEER_OFF) + b, ss)
        e = jnp.exp(ts - jnp.max(ts, axis=0, keepdims=True))
        dst = pl.ds(pl.multiple_of(hh * k, k), k)
        g_ref[c, dst, :] = e / jnp.sum(e, axis=0, keepdims=True)
        i_ref[c, dst, :] = ii
        j_ref[c, dst, :] = jj
        s_ref[c, dst, :] = ss
        li_ref[c, dst, :] = si0
        lj_ref[c, dst, :] = si1
        return 0

    lax.fori_loop(0, nchunk * PEER_HEADS, stage2, 0)


def peer_route(x, shift, scale, wq, sk, *, per_row, rows_per_batch, tm):
    n, d = x.shape
    nchunk = tm // LANES
    npick = PEER_HEADS * PEER_TOPK
    nrow = sk.shape[0]
    pick = lambda dt: jax.ShapeDtypeStruct((n // LANES, npick, LANES), dt)
    pick_spec = pl.BlockSpec((nchunk, npick, LANES), lambda i: (i, 0, 0))
    outs = pl.pallas_call(
        _peer_route_kernel,
        out_shape=(jax.ShapeDtypeStruct((n, d), BF16), pick(F32), pick(I32), pick(I32), pick(I32), pick(I32), pick(I32)),
        grid=(n // tm,),
        in_specs=[pl.BlockSpec((tm, d), lambda i: (i, 0)),
                  _mod_specs(per_row, tm, d, rows_per_batch),
                  _mod_specs(per_row, tm, d, rows_per_batch),
                  pl.BlockSpec(wq.shape, lambda i: (0, 0)),
                  pl.BlockSpec(sk.shape, lambda i: (0, 0))],
        out_specs=(pl.BlockSpec((tm, d), lambda i: (i, 0)),) + (pick_spec,) * 6,
        scratch_shapes=[pltpu.VMEM((nchunk, nrow, LANES), F32),
                        pltpu.VMEM((nchunk, 2 * npick, LANES), F32),
                        pltpu.VMEM((nchunk, 2 * npick, LANES), I32)],
        compiler_params=_cparams(("arbitrary",)),
        name="peer_route",
    )(x, shift, scale, wq, sk)
    to_rows = lambda a: jnp.swapaxes(a, 1, 2).reshape(n, npick)
    return (outs[0],) + tuple(to_rows(a) for a in outs[1:])


def _peer_subkey_matrix(subkeys):
    h, p, nk, dk = subkeys.shape
    eye = jnp.eye(h * p, dtype=F32)
    m = eye[:, None, :, None] * subkeys.reshape(h * p, nk, 1, dk)
    return m.reshape(h * p * nk, h * p * dk).astype(BF16)


PEER_EB = 1024
PEER_RC = 32


def _gather_lanes(x, idx):
    return jnp.take_along_axis(x, idx, axis=1, mode="promise_in_bounds")


def _peer_up_kernel(hb_ref, u_ref, i_ref, j_ref, g_ref, c_ref, a_ref, acc_ref):
    s = pl.program_id(1)
    tm = hb_ref.shape[0]
    nslab = PEER_EB // LANES

    @pl.when(s == 0)
    def _():
        acc_ref[...] = jnp.zeros(acc_ref.shape, F32)

    a_ref[...] = _dot_nt(hb_ref[...], u_ref[...])

    def chunk(r, _):
        rows = pl.ds(pl.multiple_of(r * PEER_RC, PEER_RC), PEER_RC)
        ii = i_ref[rows, :]
        jj = j_ref[rows, :]
        acc = acc_ref[rows, :]
        for sl in range(nslab):
            got = _gather_lanes(a_ref[rows, sl * LANES:(sl + 1) * LANES], jj)
            acc = acc + jnp.where(ii == s * nslab + sl, got, 0.0)
        acc_ref[rows, :] = acc
        return 0

    lax.fori_loop(0, tm // PEER_RC, chunk, 0)

    @pl.when(s == pl.num_programs(1) - 1)
    def _():
        a = acc_ref[...]
        c_ref[...] = g_ref[...] * (0.5 * a * (1.0 + lax.erf(a * float(2.0 ** -0.5))))


def peer_up(hb, u, ii, jj, g, *, tm):
    n, d = hb.shape
    npick = ii.shape[1]
    pick_spec = pl.BlockSpec((tm, npick), lambda i, s: (i, 0))
    return pl.pallas_call(
        _peer_up_kernel,
        out_shape=jax.ShapeDtypeStruct((n, npick), F32),
        grid=(n // tm, u.shape[0] // PEER_EB),
        in_specs=[pl.BlockSpec((tm, d), lambda i, s: (i, 0)),
                  pl.BlockSpec((PEER_EB, d), lambda i, s: (s, 0)),
                  pick_spec, pick_spec, pick_spec],
        out_specs=pick_spec,
        scratch_shapes=[pltpu.VMEM((tm, PEER_EB), F32), pltpu.VMEM((tm, npick), F32)],
        compiler_params=_cparams(("arbitrary", "arbitrary")),
        name="peer_up",
    )(hb, u, ii, jj, g)


PEER_TG = 16


def _peer_down_kernel(x_ref, gate_ref, c_ref, i_ref, j_ref, v_ref, y_ref, wa_ref, stage_ref, acc_ref):
    s = pl.program_id(1)
    tm = x_ref.shape[0]
    nk = PEER_NKEYS
    nslab = PEER_EB // LANES

    @pl.when(s == 0)
    def _():
        acc_ref[...] = jnp.zeros(acc_ref.shape, F32)
        key = lax.broadcasted_iota(I32, (nk, LANES), 0)

        def group(gi, _):
            t0 = pl.multiple_of(gi * PEER_TG, PEER_TG)
            for tl in range(PEER_TG):
                row = pl.ds(t0 + tl, 1)
                ei = jnp.where(key == i_ref[row, :], 1.0, 0.0).astype(BF16)
                cj = jnp.where(key == j_ref[row, :], c_ref[row, :], 0.0).astype(BF16)
                stage_ref[tl * nk:(tl + 1) * nk, :] = _dot_nt(ei, cj)
            for i in range(nk):
                slab = stage_ref[pl.ds(i, PEER_TG, stride=nk), :]
                wa_ref[i // nslab, pl.ds(t0, PEER_TG), (i % nslab) * LANES:(i % nslab + 1) * LANES] = slab.astype(BF16)
            return 0

        lax.fori_loop(0, tm // PEER_TG, group, 0)

    acc_ref[...] += _dot(wa_ref[s], v_ref[...])

    @pl.when(s == pl.num_programs(1) - 1)
    def _():
        y_ref[...] = x_ref[...] + gate_ref[...] * acc_ref[...]


def peer_down(x, gate, c, ii, jj, v, *, per_row, rows_per_batch, tm):
    n, d = x.shape
    npick = c.shape[1]
    nsteps = v.shape[0] // PEER_EB
    pick_spec = pl.BlockSpec((tm, npick), lambda i, s: (i, 0))
    gate_spec = (pl.BlockSpec((tm, d), lambda i, s: (i, 0)) if per_row else
                 pl.BlockSpec((None, 1, d), lambda i, s: ((i * tm) // rows_per_batch, 0, 0)))
    return pl.pallas_call(
        _peer_down_kernel,
        out_shape=jax.ShapeDtypeStruct((n, d), F32),
        grid=(n // tm, nsteps),
        in_specs=[pl.BlockSpec((tm, d), lambda i, s: (i, 0)), gate_spec,
                  pick_spec, pick_spec, pick_spec,
                  pl.BlockSpec((PEER_EB, d), lambda i, s: (s, 0))],
        out_specs=pl.BlockSpec((tm, d), lambda i, s: (i, 0)),
        scratch_shapes=[pltpu.VMEM((nsteps, tm, PEER_EB), BF16),
                        pltpu.VMEM((PEER_TG * PEER_NKEYS, LANES), F32),
                        pltpu.VMEM((tm, d), F32)],
        compiler_params=_cparams(("arbitrary", "arbitrary")),
        name="peer_down",
    )(x, gate, c, ii, jj, v)


def peer_sublayer(x, shift, scale, gate, wq, sk, u, v, *, per_row, rows_per_batch, tm_up, tm_down):
    hb, g, ii, jj, slots, li, lj = peer_route(x, shift, scale, wq, sk, per_row=per_row,
                                              rows_per_batch=rows_per_batch, tm=tm_up)
    c = peer_up(hb, u, ii, jj, g, tm=tm_up)
    return peer_down(x, gate, c, ii, jj, v, per_row=per_row, rows_per_batch=rows_per_batch, tm=tm_down)


TM_PROJ = 256
TM_PEER = 512
TM_PEER_DOWN = 256


def kernel(x_prompt, x_sample, c_prompt, c_sample, cache_mla, cache_dil0, cache_dil1, cache_dil2, page_table, a_mod_w, a_mod_b, a_w_dq, a_g_cq, a_w_uq, a_w_dkv, a_g_ckv, a_g_qn, a_g_qr, a_g_kr, a_w_uk, a_g_kn, a_w_uv, a_w_o, kv_mod_w, kv_mod_b, kv_w, kv_g_k, b_mod_w, b_mod_b, b_w_q, b_g_q, b_w_o, f_mod_w, f_mod_b, f_w_q, f_subkeys, f_u, f_v):
    b, s, d = x_prompt.shape
    db, t, _ = x_sample.shape
    n_p, n_s = b * s, db * t
    past = page_table.shape[1] * PAGE_SIZE
    depth = f_mod_w.shape[0]
    n_a = a_mod_w.shape[0]
    caches = (cache_dil0, cache_dil1, cache_dil2)
    hw = DIL_HEADS * DIL_HEAD_DIM
    lat_w = cache_mla.shape[-1]

    c_all = jnp.concatenate([c_prompt, c_sample], axis=0)
    xp = x_prompt.reshape(n_p, d)
    xs = x_sample.reshape(n_s, d)
    pos_p = jnp.arange(s, dtype=I32)
    pos_s = jnp.tile(past + jnp.arange(t, dtype=I32), db)
    mla_tab_p = _rope_tables(pos_p, MLA_SLAB_ROPE0, MLA_ROPE // 2)
    mla_tab_s = _rope_tables(pos_s, MLA_SLAB_ROPE0, MLA_ROPE // 2)
    dil_tab_p = _rope_tables(pos_p, 0, DIL_ROT // 2)
    dil_tab_s = _rope_tables(pos_s, 0, DIL_ROT // 2)
    kw_p = dict(per_row=False, rows_per_batch=s)
    kw_s = dict(per_row=True, rows_per_batch=t)

    def mods(w, bias):
        m = modulation(c_all, w, bias)
        ms = jnp.repeat(m[b:], t, axis=0)
        k = w.shape[1] // d
        return ([m[:b, None, i * d:(i + 1) * d] for i in range(k)], [ms[:, i * d:(i + 1) * d] for i in range(k)])

    rows_p, rows_s = [], []
    dil_p, dil_s = [], []
    kvb_p = kv_new = None
    for layer in range(depth):
        if layer < n_a:
            i = layer
            (sh_p, sc_p, gt_p), (sh_s, sc_s, gt_s) = mods(a_mod_w[i], a_mod_b[i])
            w = _mla_weights(a_w_dq[i], a_g_cq[i], a_w_uq[i], a_w_dkv[i], a_g_ckv[i], a_g_qn[i], a_g_qr[i],
                             a_g_kr[i], a_w_uk[i], a_g_kn[i], a_w_uv[i])
            wo = a_w_o[i].astype(BF16)
            q, k, v, ckv, kpe = mla_project(xp, sh_p, sc_p, *mla_tab_p, w, tm=TM_PROJ, **kw_p)
            rows_p.append(jnp.concatenate([ckv, kpe[:, MLA_NOPE:MLA_NOPE + MLA_ROPE]], axis=1).reshape(b, s, lat_w))
            o = mla_attention(q, k, v, b, s)
            xp = out_proj(xp, o, wo, gt_p, tm=TM_PROJ, **kw_p)
            q, k, v, ckv, kpe = mla_project(xs, sh_s, sc_s, *mla_tab_s, w, tm=n_s, **kw_s)
            rows_s.append(jnp.concatenate([ckv, kpe[:, MLA_NOPE:MLA_NOPE + MLA_ROPE]], axis=1).reshape(db, t, lat_w))
            o = mla_sample_attention(q, ckv, kpe, cache_mla[i], page_table, a_w_uk[i], a_g_kn[i], a_w_uv[i], t)
            xs = out_proj(xs, o, wo, gt_s, tm=n_s, **kw_s)
        else:
            if layer == n_a:
                (sh_p, sc_p), (sh_s, sc_s) = mods(kv_mod_w, kv_mod_b)
                kvw = kv_w.reshape(d, 2, N_GROUPS, hw)
                wkv = jnp.stack([jnp.concatenate([kvw[:, 0, g], kvw[:, 1, g]], axis=1) for g in range(N_GROUPS)])
                wkv = wkv.astype(BF16)
                gk = kv_g_k[:, None, :]
                kvf_p, kvb_p = ada_heads(xp, sh_p, sc_p, *dil_tab_p, wkv, gk, n_norm=DIL_HEADS, oscale=1.0,
                                         emit_f32=True, tm=TM_PROJ, **kw_p)
                kvf_s, _ = ada_heads(xs, sh_s, sc_s, *dil_tab_s, wkv, gk, n_norm=DIL_HEADS, oscale=1.0,
                                     emit_f32=True, tm=n_s, **kw_s)
                new_p = kvf_p.reshape(b, s, N_GROUPS, 2, DIL_HEADS, DIL_HEAD_DIM)
                new_s = kvf_s.reshape(db, t, N_GROUPS, 2, DIL_HEADS, DIL_HEAD_DIM)
                for g in range(N_GROUPS):
                    win = DIL_WINDOWS[g]
                    full_s = jnp.concatenate([caches[g], new_s[:, :, g]], axis=1)
                    dil_p.append(new_p[:, s - min(win, s):, g])
                    dil_s.append(full_s[:, full_s.shape[1] - min(win, past + t):])
                kv_new = jnp.pad(kvf_s.reshape(db, t, -1), ((0, 0), (0, MLA_DEC_ROWS - t), (0, 0)))
            j = layer - n_a
            (sh_p, sc_p, gt_p), (sh_s, sc_s, gt_s) = mods(b_mod_w[j], b_mod_b[j])
            wq = jnp.transpose(b_w_q[j].reshape(d, N_GROUPS, hw), (1, 0, 2)).astype(BF16)
            gq = b_g_q[j][:, None, :]
            wo = b_w_o[j].astype(BF16)
            qscale = float(DIL_HEAD_DIM ** -0.5)
            (qb,) = ada_heads(xp, sh_p, sc_p, *dil_tab_p, wq, gq, n_norm=DIL_HEADS, oscale=qscale,
                              emit_f32=False, tm=TM_PROJ, **kw_p)
            parts = [dil_prompt_attention(qb, kvb_p, g, b, s) for g in range(N_GROUPS)]
            xp = dil_combine(xp, [p[0] for p in parts], [p[1] for p in parts], wo, gt_p, tm=TM_PROJ, **kw_p)
            (qb,) = ada_heads(xs, sh_s, sc_s, *dil_tab_s, wq, gq, n_norm=DIL_HEADS, oscale=qscale,
                              emit_f32=False, tm=n_s, **kw_s)
            parts = [dil_sample_attention(qb, caches[g], kv_new, g, t) for g in range(N_GROUPS)]
            xs = dil_combine(xs, [p[0] for p in parts], [p[1] for p in parts], wo, gt_s, tm=n_s, **kw_s)
        (sh_p, sc_p, gt_p), (sh_s, sc_s, gt_s) = mods(f_mod_w[layer], f_mod_b[layer])
        wq = f_w_q[layer].astype(BF16)
        sk = _peer_subkey_matrix(f_subkeys[layer])
        u = f_u[layer].astype(BF16)
        v = f_v[layer].astype(BF16)
        xp = peer_sublayer(xp, sh_p, sc_p, gt_p, wq, sk, u, v, tm_up=TM_PEER, tm_down=TM_PEER_DOWN, **kw_p)
        xs = peer_sublayer(xs, sh_s, sc_s, gt_s, wq, sk, u, v, tm_up=n_s, tm_down=n_s, **kw_s)
    return (xp.reshape(b, s, d), xs.reshape(db, t, d), jnp.stack(rows_p), jnp.stack(rows_s),
            dil_p[0], dil_s[0], dil_p[1], dil_s[1], dil_p[2], dil_s[2])
```

```python
import functools

import numpy as np
import jax
import jax.numpy as jnp
from jax import lax
from jax.experimental import pallas as pl
from jax.experimental.pallas import tpu as pltpu

F32 = jnp.float32
BF16 = jnp.bfloat16
I32 = jnp.int32

LANES = 128
VMEM_LIMIT = 56 * 1024 * 1024

ROPE_THETA = 500000.0
EPS = 1e-6
NEG = -1e30

MLA_HEADS = 16
MLA_NOPE = 64
MLA_ROPE = 32
MLA_V = 64
MLA_KV_LORA = 256
PAGE_SIZE = 128

DIL_WINDOWS = (128, 512, 2048)
DIL_RATES = (1, 4, 16)
N_GROUPS = 3
DIL_HEADS = 8
DIL_HEAD_DIM = 128
DIL_ROT = DIL_HEAD_DIM // 4

PEER_HEADS = 8
PEER_NKEYS = 128
PEER_TOPK = 16
PEER_DKEY = 128


def _cparams(sem):
    return pltpu.CompilerParams(dimension_semantics=sem, vmem_limit_bytes=VMEM_LIMIT)


def _rms(x):
    return x * lax.rsqrt(jnp.mean(x * x, axis=-1, keepdims=True) + EPS)


def _dot(a, b):
    return jnp.dot(a, b, preferred_element_type=F32)


def _dot_nt(a, b):
    return lax.dot_general(a, b, (((1,), (1,)), ((), ())), preferred_element_type=F32)


def _mod_specs(per_row, tm, d, rows_per_batch):
    if per_row:
        return pl.BlockSpec((tm, d), lambda i: (i, 0))
    return pl.BlockSpec((None, 1, d), lambda i: ((i * tm) // rows_per_batch, 0, 0))


def _mod_kernel(c_ref, w_ref, b_ref, o_ref):
    c = c_ref[...]
    s = (c * jax.nn.sigmoid(c)).astype(BF16)
    o_ref[...] = _dot(s, w_ref[...].astype(BF16)) + b_ref[...]


def modulation(c, w, b):
    bc, d = c.shape
    n = w.shape[1]
    tn = 1024
    return pl.pallas_call(
        _mod_kernel,
        out_shape=jax.ShapeDtypeStruct((bc, n), F32),
        grid=(n // tn,),
        in_specs=[pl.BlockSpec((bc, d), lambda j: (0, 0)),
                  pl.BlockSpec((d, tn), lambda j: (0, j)),
                  pl.BlockSpec((1, tn), lambda j: (0, j))],
        out_specs=pl.BlockSpec((bc, tn), lambda j: (0, j)),
        compiler_params=_cparams(("arbitrary",)),
        name="modulation",
    )(c, w, b.reshape(1, n))


def _rope_tables(pos, lane0, half):
    inv = ROPE_THETA ** (-jnp.arange(half, dtype=F32) / half)
    ang = pos.astype(F32)[:, None] * inv
    cos, sin = jnp.cos(ang), jnp.sin(ang)
    t = pos.shape[0]
    cos_t = jnp.ones((t, LANES), F32)
    sin_t = jnp.zeros((t, LANES), F32)
    cos_t = cos_t.at[:, lane0:lane0 + half].set(cos).at[:, lane0 + half:lane0 + 2 * half].set(cos)
    sin_t = sin_t.at[:, lane0:lane0 + half].set(-sin).at[:, lane0 + half:lane0 + 2 * half].set(sin)
    return cos_t, sin_t


def _rope_slab(s, cos_t, sin_t, lane, lane0, half):
    swapped = jnp.where(lane < lane0 + half, pltpu.roll(s, LANES - half, 1), pltpu.roll(s, half, 1))
    return s * cos_t + swapped * sin_t


MLA_SLAB_ROPE0 = MLA_NOPE


def _mla_proj_kernel(x_ref, sh_ref, sc_ref, cos_ref, sin_ref, wdq_ref, gcq_ref, wuq_ref, wdkv_ref, gckv_ref,
                     gq_ref, gkr_ref, wuk_ref, gkn_ref, wuv_ref,
                     q_ref, k_ref, v_ref, ckv_ref, kpe_ref, *, qscale):
    x = x_ref[...]
    tm = x.shape[0]
    h = _rms(x) * (1.0 + sc_ref[...]) + sh_ref[...]
    hb = h.astype(BF16)
    cq = _rms(_dot(hb, wdq_ref[...])) * gcq_ref[...]
    q = _dot(cq.astype(BF16), wuq_ref[...])
    kv = _dot(hb, wdkv_ref[...])
    ckv = _rms(kv[:, :MLA_KV_LORA]) * gckv_ref[...]
    cos_t, sin_t = cos_ref[...], sin_ref[...]
    lane = lax.broadcasted_iota(I32, (tm, LANES), 1)
    m_n = lane < MLA_NOPE
    m_r = (lane >= MLA_NOPE) & (lane < MLA_NOPE + MLA_ROPE)

    def norm_rope(s, g):
        s2 = s * s
        ssn = jnp.sum(jnp.where(m_n, s2, 0.0), axis=-1, keepdims=True)
        ssr = jnp.sum(jnp.where(m_r, s2, 0.0), axis=-1, keepdims=True)
        r = jnp.where(m_n, lax.rsqrt(ssn / MLA_NOPE + EPS), lax.rsqrt(ssr / MLA_ROPE + EPS))
        return _rope_slab(s * r * g, cos_t, sin_t, lane, MLA_SLAB_ROPE0, MLA_ROPE // 2)

    kpe = norm_rope(kv[:, MLA_KV_LORA:], gkr_ref[...])
    ckv_ref[...] = ckv
    kpe_ref[...] = kpe
    ckvb = ckv.astype(BF16)
    kn = _dot(ckvb, wuk_ref[...])
    v_ref[...] = _dot(ckvb, wuv_ref[...]).astype(BF16)
    gq = gq_ref[...]
    gkn = gkn_ref[...]
    for hh in range(MLA_HEADS):
        sl = slice(hh * LANES, (hh + 1) * LANES)
        q_ref[hh] = (norm_rope(q[:, sl], gq) * qscale).astype(BF16)
        ks = kn[:, sl]
        ssn = jnp.sum(ks * ks, axis=-1, keepdims=True)
        k_ref[hh] = (ks * lax.rsqrt(ssn / MLA_NOPE + EPS) * gkn + kpe).astype(BF16)


def _mla_weights(w_dq, g_cq, w_uq, w_dkv, g_ckv, g_qn, g_qr, g_kr, w_uk, g_kn, w_uv):
    d = w_dq.shape[0]
    hd = MLA_NOPE + MLA_ROPE
    pad = LANES - hd
    wuq = jnp.pad(w_uq.reshape(-1, MLA_HEADS, hd), ((0, 0), (0, 0), (0, pad))).reshape(-1, MLA_HEADS * LANES)
    wdkv = jnp.concatenate([w_dkv[:, :MLA_KV_LORA], jnp.zeros((d, MLA_NOPE), F32), w_dkv[:, MLA_KV_LORA:],
                            jnp.zeros((d, pad), F32)], axis=1)
    gq = jnp.concatenate([g_qn, g_qr, jnp.zeros((pad,), F32)])[None]
    gkr = jnp.concatenate([jnp.zeros((MLA_NOPE,), F32), g_kr, jnp.zeros((pad,), F32)])[None]
    wuk = jnp.pad(w_uk, ((0, 0), (0, 0), (0, LANES - MLA_NOPE))).reshape(MLA_KV_LORA, MLA_HEADS * LANES)
    gkn = jnp.concatenate([g_kn, jnp.zeros((LANES - MLA_NOPE,), F32)])[None]
    wuv = w_uv.reshape(MLA_KV_LORA, MLA_HEADS * MLA_V)
    return dict(wdq=w_dq.astype(BF16), gcq=g_cq[None], wuq=wuq.astype(BF16), wdkv=wdkv.astype(BF16),
                gckv=g_ckv[None], gq=gq, gkr=gkr, wuk=wuk.astype(BF16), gkn=gkn, wuv=wuv.astype(BF16))


def mla_project(x, shift, scale, cos_t, sin_t, w, *, per_row, rows_per_batch, tm):
    n, d = x.shape
    nblk_pos = cos_t.shape[0] // tm
    full = lambda a: pl.BlockSpec(a.shape, lambda i: (0,) * a.ndim)
    wnames = ("wdq", "gcq", "wuq", "wdkv", "gckv", "gq", "gkr", "wuk", "gkn", "wuv")
    qscale = float((MLA_NOPE + MLA_ROPE) ** -0.5)
    return pl.pallas_call(
        functools.partial(_mla_proj_kernel, qscale=qscale),
        out_shape=(jax.ShapeDtypeStruct((MLA_HEADS, n, LANES), BF16),
                   jax.ShapeDtypeStruct((MLA_HEADS, n, LANES), BF16),
                   jax.ShapeDtypeStruct((n, MLA_HEADS * MLA_V), BF16),
                   jax.ShapeDtypeStruct((n, MLA_KV_LORA), F32),
                   jax.ShapeDtypeStruct((n, LANES), F32)),
        grid=(n // tm,),
        in_specs=[pl.BlockSpec((tm, d), lambda i: (i, 0)),
                  _mod_specs(per_row, tm, d, rows_per_batch),
                  _mod_specs(per_row, tm, d, rows_per_batch),
                  pl.BlockSpec((tm, LANES), lambda i: (i % nblk_pos, 0)),
                  pl.BlockSpec((tm, LANES), lambda i: (i % nblk_pos, 0))] + [full(w[k]) for k in wnames],
        out_specs=(pl.BlockSpec((MLA_HEADS, tm, LANES), lambda i: (0, i, 0)),
                   pl.BlockSpec((MLA_HEADS, tm, LANES), lambda i: (0, i, 0)),
                   pl.BlockSpec((tm, MLA_HEADS * MLA_V), lambda i: (i, 0)),
                   pl.BlockSpec((tm, MLA_KV_LORA), lambda i: (i, 0)),
                   pl.BlockSpec((tm, LANES), lambda i: (i, 0))),
        compiler_params=_cparams(("arbitrary",)),
        name="mla_project",
    )(x, shift, scale, cos_t, sin_t, *[w[k] for k in wnames])


def _mla_attn_kernel(q_ref, k_ref, v_ref, o_ref, *, tq):
    qi = pl.program_id(2)
    row = lax.broadcasted_iota(I32, (tq, tq), 0)
    col = lax.broadcasted_iota(I32, (tq, tq), 1)
    qs = (q_ref[0], q_ref[1])

    def step(j, carry, masked):
        off = pl.multiple_of(j * tq, tq)
        vc = v_ref[pl.ds(off, tq), :]
        out = []
        for hh in range(2):
            m, l, acc = carry[hh]
            s = _dot_nt(qs[hh], k_ref[hh, pl.ds(off, tq), :])
            if masked:
                s = jnp.where(col <= row, s, NEG)
            m_new = jnp.maximum(m, jnp.max(s, axis=-1, keepdims=True))
            alpha = jnp.exp(m - m_new)
            p = jnp.exp(s - m_new)
            l = alpha * l + jnp.sum(p, axis=-1, keepdims=True)
            acc = alpha * acc + _dot(p.astype(BF16), vc)
            out.append((m_new, l, acc))
        return tuple(out)

    init = (jnp.full((tq, 1), NEG, F32), jnp.zeros((tq, 1), F32), jnp.zeros((tq, LANES), F32))
    carry = lax.fori_loop(0, qi, functools.partial(step, masked=False), (init, init))
    carry = step(qi, carry, masked=True)
    lane = lax.broadcasted_iota(I32, (tq, LANES), 1)
    o_ref[...] = jnp.where(lane < MLA_V, carry[0][2] / carry[0][1], carry[1][2] / carry[1][1]).astype(BF16)


def mla_attention(q, k, v, b, s):
    tq = 256
    hp = MLA_HEADS // 2
    q4 = q.reshape(MLA_HEADS, b, s, LANES)
    k4 = k.reshape(MLA_HEADS, b, s, LANES)
    v3 = v.reshape(b, s, MLA_HEADS * MLA_V)
    out = pl.pallas_call(
        functools.partial(_mla_attn_kernel, tq=tq),
        out_shape=jax.ShapeDtypeStruct((b, s, MLA_HEADS * MLA_V), BF16),
        grid=(b, hp, s // tq),
        in_specs=[pl.BlockSpec((2, None, tq, LANES), lambda bi, h, qi: (h, bi, qi, 0)),
                  pl.BlockSpec((2, None, s, LANES), lambda bi, h, qi: (h, bi, 0, 0)),
                  pl.BlockSpec((None, s, LANES), lambda bi, h, qi: (bi, 0, h))],
        out_specs=pl.BlockSpec((None, tq, LANES), lambda bi, h, qi: (bi, qi, h)),
        compiler_params=_cparams(("arbitrary", "arbitrary", "arbitrary")),
        name="mla_attention",
    )(q4, k4, v3)
    return out.reshape(b * s, MLA_HEADS * MLA_V)


def _out_proj_kernel(x_ref, o_ref, w_ref, g_ref, y_ref):
    y_ref[...] = x_ref[...] + g_ref[...] * _dot(o_ref[...], w_ref[...])


def out_proj(x, o, w, gate, *, per_row, rows_per_batch, tm):
    n, d = x.shape
    kdim = o.shape[1]
    return pl.pallas_call(
        _out_proj_kernel,
        out_shape=jax.ShapeDtypeStruct((n, d), F32),
        grid=(n // tm,),
        in_specs=[pl.BlockSpec((tm, d), lambda i: (i, 0)),
                  pl.BlockSpec((tm, kdim), lambda i: (i, 0)),
                  pl.BlockSpec((kdim, d), lambda i: (0, 0)),
                  _mod_specs(per_row, tm, d, rows_per_batch)],
        out_specs=pl.BlockSpec((tm, d), lambda i: (i, 0)),
        compiler_params=_cparams(("arbitrary",)),
        name="out_proj",
    )(x, o, w, gate)


MLA_DEC_PAGES = 8
NEW_ROWS = 8


def _qabs_kernel(q_ref, w_ref, o_ref):
    o_ref[...] = _dot(q_ref[...], w_ref[...]).astype(BF16)


def mla_absorb_q(q, w_uk, g_kn):
    h, n, _ = q.shape
    wt = jnp.transpose(w_uk, (1, 2, 0)) * g_kn[None, :, None]
    wt = jnp.pad(wt, ((0, 0), (0, LANES - MLA_NOPE), (0, 0))).astype(BF16)
    return pl.pallas_call(
        _qabs_kernel,
        out_shape=jax.ShapeDtypeStruct((h, n, MLA_KV_LORA), BF16),
        grid=(h,),
        in_specs=[pl.BlockSpec((None, n, LANES), lambda i: (i, 0, 0)),
                  pl.BlockSpec((None, LANES, MLA_KV_LORA), lambda i: (i, 0, 0))],
        out_specs=pl.BlockSpec((None, n, MLA_KV_LORA), lambda i: (i, 0, 0)),
        compiler_params=_cparams(("arbitrary",)),
        name="mla_absorb_q",
    )(q, wt)


def _mla_decode_kernel(pt_ref, qa_ref, qp_ref, new_ref, wukt_ref, *rest, n_dec):
    page_refs = rest[:MLA_DEC_PAGES]
    o_ref = rest[MLA_DEC_PAGES]
    m_ref, l_ref, acc_ref = rest[MLA_DEC_PAGES + 1:]
    j = pl.program_id(1)
    qa = qa_ref[...]
    qp = qp_ref[...]
    nrow = qa.shape[0]

    def scores(lat_t):
        n = lat_t.shape[1]
        ckvt = lat_t[:MLA_KV_LORA].astype(BF16)
        kpet = lat_t[MLA_KV_LORA:].astype(BF16)
        kn = _dot(wukt_ref[...], ckvt)
        ssq = jnp.sum((kn * kn).reshape(MLA_HEADS, MLA_NOPE, n), axis=1)
        r = lax.rsqrt(ssq / MLA_NOPE + EPS)
        rr = jnp.concatenate([r] * n_dec + [jnp.zeros((nrow - n_dec * MLA_HEADS, n), F32)], axis=0)
        return _dot(qa, ckvt) * rr + _dot(qp, kpet), ckvt

    def accumulate(s, ckvt):
        m = m_ref[...]
        m_new = jnp.maximum(m, jnp.max(s, axis=-1, keepdims=True))
        alpha = jnp.exp(m - m_new)
        p = jnp.exp(s - m_new)
        l_ref[...] = alpha * l_ref[...] + jnp.sum(p, axis=-1, keepdims=True)
        acc_ref[...] = alpha * acc_ref[...] + _dot_nt(p.astype(BF16), ckvt)
        m_ref[...] = m_new

    @pl.when(j == 0)
    def _():
        m_ref[...] = jnp.full(m_ref.shape, NEG, F32)
        l_ref[...] = jnp.zeros(l_ref.shape, F32)
        acc_ref[...] = jnp.zeros(acc_ref.shape, F32)
        s, ckvt = scores(new_ref[...])
        key = lax.broadcasted_iota(I32, s.shape, 1)
        tok = lax.broadcasted_iota(I32, s.shape, 0) // MLA_HEADS
        accumulate(jnp.where((key <= tok) & (key < n_dec), s, NEG), ckvt)

    s, ckvt = scores(jnp.concatenate([r[...] for r in page_refs], axis=1))
    accumulate(s, ckvt)

    @pl.when(j == pl.num_programs(1) - 1)
    def _():
        o_ref[...] = acc_ref[...] / l_ref[...]


def mla_decode_attention(qabs, qpe, lat_new_t, pool_t, page_table, w_uk, n_dec):
    db, n_pages = page_table.shape
    lat = pool_t.shape[1]
    nrow = qabs.shape[1]
    wukt = w_uk.reshape(MLA_KV_LORA, MLA_HEADS * MLA_NOPE).T.astype(BF16)
    steps = n_pages // MLA_DEC_PAGES
    page_specs = [pl.BlockSpec((None, lat, PAGE_SIZE),
                               lambda b, j, pt, pp=pp: (pt[b * n_pages + j * MLA_DEC_PAGES + pp], 0, 0))
                  for pp in range(MLA_DEC_PAGES)]
    grid_spec = pltpu.PrefetchScalarGridSpec(
        num_scalar_prefetch=1,
        grid=(db, steps),
        in_specs=[pl.BlockSpec((None, nrow, MLA_KV_LORA), lambda b, j, pt: (b, 0, 0)),
                  pl.BlockSpec((None, nrow, MLA_ROPE), lambda b, j, pt: (b, 0, 0)),
                  pl.BlockSpec((None, lat, LANES), lambda b, j, pt: (b, 0, 0)),
                  pl.BlockSpec(wukt.shape, lambda b, j, pt: (0, 0))] + page_specs,
        out_specs=pl.BlockSpec((None, nrow, MLA_KV_LORA), lambda b, j, pt: (b, 0, 0)),
        scratch_shapes=[pltpu.VMEM((nrow, 1), F32), pltpu.VMEM((nrow, 1), F32),
                        pltpu.VMEM((nrow, MLA_KV_LORA), F32)],
    )
    return pl.pallas_call(
        functools.partial(_mla_decode_kernel, n_dec=n_dec),
        out_shape=jax.ShapeDtypeStruct((db, nrow, MLA_KV_LORA), F32),
        grid_spec=grid_spec,
        compiler_params=_cparams(("arbitrary", "arbitrary")),
        name="mla_decode_attention",
    )(page_table.reshape(-1), qabs, qpe, lat_new_t, wukt, *([pool_t] * MLA_DEC_PAGES))


def _mla_uv_kernel(x_ref, w_ref, o_ref):
    w = w_ref[...]
    lane = lax.broadcasted_iota(I32, o_ref.shape, 1)
    o_ref[...] = jnp.where(lane < MLA_V, _dot(x_ref[0], w), _dot(x_ref[1], w)).astype(BF16)


def mla_decode_values(x, w_uv):
    h, n, _ = x.shape
    wuv = w_uv.reshape(MLA_KV_LORA, MLA_HEADS * MLA_V).astype(BF16)
    return pl.pallas_call(
        _mla_uv_kernel,
        out_shape=jax.ShapeDtypeStruct((n, MLA_HEADS * MLA_V), BF16),
        grid=(h // 2,),
        in_specs=[pl.BlockSpec((2, n, MLA_KV_LORA), lambda i: (i, 0, 0)),
                  pl.BlockSpec((MLA_KV_LORA, LANES), lambda i: (0, i))],
        out_specs=pl.BlockSpec((n, LANES), lambda i: (0, i)),
        compiler_params=_cparams(("arbitrary",)),
        name="mla_decode_values",
    )(x, wuv)


def mla_sample_attention(q, ckv, kpe, pool, page_table, w_uk, g_kn, w_uv, n_dec):
    db = page_table.shape[0]
    n = db * n_dec
    qabs = mla_absorb_q(q, w_uk, g_kn)
    nq = MLA_HEADS * n_dec
    assert nq <= LANES

    def to_rows(a):
        f = a.shape[-1]
        a = a.reshape(MLA_HEADS, db, n_dec, f).transpose(1, 2, 0, 3).reshape(db, nq, f)
        return jnp.pad(a, ((0, 0), (0, LANES - nq), (0, 0)))

    lat_new = jnp.concatenate([ckv, kpe[:, MLA_NOPE:MLA_NOPE + MLA_ROPE]], axis=1).reshape(db, n_dec, -1)
    lat_new_t = jnp.pad(jnp.swapaxes(lat_new, 1, 2), ((0, 0), (0, 0), (0, LANES - n_dec)))
    out = mla_decode_attention(to_rows(qabs), to_rows(q[:, :, MLA_NOPE:MLA_NOPE + MLA_ROPE]), lat_new_t,
                               jnp.swapaxes(pool, 1, 2), page_table, w_uk, n_dec)
    x = out[:, :nq].reshape(db, n_dec, MLA_HEADS, MLA_KV_LORA).transpose(2, 0, 1, 3)
    x = x.reshape(MLA_HEADS, n, MLA_KV_LORA).astype(BF16)
    return mla_decode_values(x, w_uv)


def _ada_heads_kernel(x_ref, sh_ref, sc_ref, cos_ref, sin_ref, w_ref, g_ref, o_ref, *, n_norm, n_slab, oscale):
    x = x_ref[...]
    tm = x.shape[0]
    h = _rms(x) * (1.0 + sc_ref[...]) + sh_ref[...]
    y = _dot(h.astype(BF16), w_ref[...])
    cos_t, sin_t = cos_ref[...], sin_ref[...]
    g = g_ref[...]
    lane = lax.broadcasted_iota(I32, (tm, LANES), 1)
    for s in range(n_slab):
        sl = slice(s * LANES, (s + 1) * LANES)
        ys = y[:, sl]
        if s < n_norm:
            ys = _rope_slab(_rms(ys) * g, cos_t, sin_t, lane, 0, DIL_ROT // 2) * oscale
        o_ref[:, sl] = ys


def ada_heads(x, shift, scale, cos_t, sin_t, w, g, *, n_norm, oscale, per_row, rows_per_batch, tm):
    n, d = x.shape
    ng, _, gw = w.shape
    n_slab = gw // LANES
    nblk_pos = cos_t.shape[0] // tm
    mod_spec = (pl.BlockSpec((tm, d), lambda gi, i: (i, 0)) if per_row else
                pl.BlockSpec((None, 1, d), lambda gi, i: ((i * tm) // rows_per_batch, 0, 0)))
    return pl.pallas_call(
        functools.partial(_ada_heads_kernel, n_norm=n_norm, n_slab=n_slab, oscale=oscale),
        out_shape=jax.ShapeDtypeStruct((n, ng * gw), F32),
        grid=(ng, n // tm),
        in_specs=[pl.BlockSpec((tm, d), lambda gi, i: (i, 0)), mod_spec, mod_spec,
                  pl.BlockSpec((tm, LANES), lambda gi, i: (i % nblk_pos, 0)),
                  pl.BlockSpec((tm, LANES), lambda gi, i: (i % nblk_pos, 0)),
                  pl.BlockSpec((None, d, gw), lambda gi, i: (gi, 0, 0)),
                  pl.BlockSpec((None, 1, LANES), lambda gi, i: (gi, 0, 0))],
        out_specs=pl.BlockSpec((tm, gw), lambda gi, i: (i, gi)),
        compiler_params=_cparams(("arbitrary", "arbitrary")),
        name="ada_heads",
    )(x, shift, scale, cos_t, sin_t, w, g)


DIL_SPAN = 128
STAT_DEN0 = DIL_HEADS


def _dil_prompt_kernel(q_ref, kp_ref, vp_ref, kc_ref, vc_ref, num_ref, st_ref, *, dil, hps):
    nb = pl.program_id(1)
    hs = pl.program_id(2)
    span = DIL_SPAN
    iq = lax.broadcasted_iota(I32, (span, span), 0)
    jk = lax.broadcasted_iota(I32, (span, span), 1)
    mask_prev = (jk >= iq) & (nb > 0)
    mask_cur = jk <= iq
    lane = lax.broadcasted_iota(I32, (span, LANES), 1)

    @pl.when(hs == 0)
    def _():
        st_ref[...] = jnp.zeros(st_ref.shape, F32)

    def residue(r, _):
        rows = pl.ds(r, span, stride=dil) if dil > 1 else pl.ds(0, span)
        stats = st_ref[rows, :]
        for hl in range(hps):
            sl = slice(hl * LANES, (hl + 1) * LANES)
            q = q_ref[rows, sl].astype(BF16)
            sp = jnp.where(mask_prev, _dot_nt(q, kp_ref[rows, sl].astype(BF16)), NEG)
            sc = jnp.where(mask_cur, _dot_nt(q, kc_ref[rows, sl].astype(BF16)), NEG)
            m = jnp.maximum(jnp.max(sp, axis=-1, keepdims=True), jnp.max(sc, axis=-1, keepdims=True))
            pp = jnp.exp(sp - m)
            pc = jnp.exp(sc - m)
            den = jnp.sum(pp, axis=-1, keepdims=True) + jnp.sum(pc, axis=-1, keepdims=True)
            num_ref[rows, sl] = (_dot(pp.astype(BF16), vp_ref[rows, sl].astype(BF16))
                                 + _dot(pc.astype(BF16), vc_ref[rows, sl].astype(BF16)))
            hh = hs * hps + hl
            stats = jnp.where(lane == hh, m, stats)
            stats = jnp.where(lane == STAT_DEN0 + hh, den, stats)
        st_ref[rows, :] = stats
        return 0

    if dil > 1:
        lax.fori_loop(0, dil, residue, 0)
    else:
        residue(0, 0)


DIL_HEADS_PER_STEP = (8, 1, 1)


def dil_prompt_attention(q, kv, g, b, s):
    dil = DIL_RATES[g]
    span = DIL_WINDOWS[g] // dil
    assert span == DIL_SPAN and s % (dil * span) == 0
    rows = span * dil
    n_blk = s // rows
    hps = DIL_HEADS_PER_STEP[g]
    hw = DIL_HEADS * DIL_HEAD_DIM
    bw = hps * DIL_HEAD_DIM
    nh = DIL_HEADS // hps
    q3 = q.reshape(b, s, q.shape[1])
    kv3 = kv.reshape(b, s, kv.shape[1])
    prev = lambda nb: jnp.maximum(nb - 1, 0)
    blk = lambda row_of, col0: pl.BlockSpec((None, rows, bw), lambda bi, nb, hs: (bi, row_of(nb), col0 * nh + hs))
    cur = lambda nb: nb
    num, st = pl.pallas_call(
        functools.partial(_dil_prompt_kernel, dil=dil, hps=hps),
        out_shape=(jax.ShapeDtypeStruct((b, s, hw), F32), jax.ShapeDtypeStruct((b, s, LANES), F32)),
        grid=(b, n_blk, nh),
        in_specs=[blk(cur, g), blk(prev, 2 * g), blk(prev, 2 * g + 1), blk(cur, 2 * g), blk(cur, 2 * g + 1)],
        out_specs=(blk(cur, 0), pl.BlockSpec((None, rows, LANES), lambda bi, nb, hs: (bi, nb, 0))),
        compiler_params=_cparams(("arbitrary", "arbitrary", "arbitrary")),
        name=f"dil_prompt_attention_g{g}",
    )(q3, kv3, kv3, kv3, kv3)
    return num.reshape(b * s, hw), st.reshape(b * s, LANES)


def _dil_sample_kernel(q_ref, cache_ref, new_ref, num_ref, st_ref, *, dil, n_dec):
    t = pl.program_id(1)
    hw = DIL_HEADS * DIL_HEAD_DIM
    rows = cache_ref.shape[0]
    n_new = new_ref.shape[0]
    crow = lax.broadcasted_iota(I32, (rows, 1), 0)
    nrow = lax.broadcasted_iota(I32, (n_new, 1), 0)
    cache_ok = (crow >= t) if dil == 1 else (crow >= 0)
    new_ok = (nrow <= t) & (nrow < n_dec) if dil == 1 else (nrow == t)
    lane = lax.broadcasted_iota(I32, (1, LANES), 1)
    stats = jnp.zeros((1, LANES), F32)
    for hh in range(DIL_HEADS):
        sl = slice(hh * LANES, (hh + 1) * LANES)
        q = q_ref[:, sl]
        sc = jnp.where(cache_ok, jnp.sum(cache_ref[:, 0, hh, :] * q, axis=-1, keepdims=True), NEG)
        sn = jnp.where(new_ok, jnp.sum(new_ref[:, sl] * q, axis=-1, keepdims=True), NEG)
        m = jnp.maximum(jnp.max(sc, axis=0, keepdims=True), jnp.max(sn, axis=0, keepdims=True))
        pc = jnp.exp(sc - m)
        pn = jnp.exp(sn - m)
        den = jnp.sum(pc, axis=0, keepdims=True) + jnp.sum(pn, axis=0, keepdims=True)
        vs = slice(hw + hh * LANES, hw + (hh + 1) * LANES)
        num_ref[:, sl] = (jnp.sum(pc * cache_ref[:, 1, hh, :], axis=0, keepdims=True)
                          + jnp.sum(pn * new_ref[:, vs], axis=0, keepdims=True))
        stats = jnp.where(lane == hh, m, stats)
        stats = jnp.where(lane == STAT_DEN0 + hh, den, stats)
    st_ref[...] = stats


def dil_sample_attention(q, cache, kv_new, g, n_dec):
    dil = DIL_RATES[g]
    db, w = cache.shape[:2]
    assert w == DIL_WINDOWS[g] and w // dil == DIL_SPAN and (dil == 1 or n_dec <= dil)
    hw = DIL_HEADS * DIL_HEAD_DIM
    rows = w // dil
    cv = cache.reshape(db, rows, dil, 2, DIL_HEADS, DIL_HEAD_DIM)
    q4 = q.reshape(db, n_dec, 1, q.shape[1])
    n_new = kv_new.shape[1]
    num, st = pl.pallas_call(
        functools.partial(_dil_sample_kernel, dil=dil, n_dec=n_dec),
        out_shape=(jax.ShapeDtypeStruct((db, n_dec, 1, hw), F32),
                   jax.ShapeDtypeStruct((db, n_dec, 1, LANES), F32)),
        grid=(db, n_dec),
        in_specs=[pl.BlockSpec((None, None, 1, hw), lambda bi, t: (bi, t, 0, g)),
                  pl.BlockSpec((None, rows, None, 2, DIL_HEADS, DIL_HEAD_DIM),
                               lambda bi, t: (bi, 0, t if dil > 1 else 0, 0, 0, 0)),
                  pl.BlockSpec((None, n_new, 2 * hw), lambda bi, t: (bi, 0, g))],
        out_specs=(pl.BlockSpec((None, None, 1, hw), lambda bi, t: (bi, t, 0, 0)),
                   pl.BlockSpec((None, None, 1, LANES), lambda bi, t: (bi, t, 0, 0))),
        compiler_params=_cparams(("arbitrary", "arbitrary")),
        name=f"dil_sample_attention_g{g}",
    )(q4, cv, kv_new)
    return num.reshape(db * n_dec, hw), st.reshape(db * n_dec, LANES)


def _dil_combine_kernel(x_ref, n0_ref, n1_ref, n2_ref, s0_ref, s1_ref, s2_ref, w_ref, g_ref, y_ref):
    nums = (n0_ref, n1_ref, n2_ref)
    stats = (s0_ref[...], s1_ref[...], s2_ref[...])
    outs = []
    for hh in range(DIL_HEADS):
        sl = slice(hh * LANES, (hh + 1) * LANES)
        ms = [st[:, hh:hh + 1] for st in stats]
        ds = [st[:, STAT_DEN0 + hh:STAT_DEN0 + hh + 1] for st in stats]
        big = jnp.maximum(jnp.maximum(ms[0], ms[1]), ms[2])
        num = 0.0
        den = 0.0
        for gi in range(N_GROUPS):
            wgt = jnp.exp(ms[gi] - big)
            num = num + wgt * nums[gi][:, sl]
            den = den + wgt * ds[gi]
        outs.append((num / den).astype(BF16))
    o = jnp.concatenate(outs, axis=1)
    y_ref[...] = x_ref[...] + g_ref[...] * _dot(o, w_ref[...])


def dil_combine(x, nums, stats, w, gate, *, per_row, rows_per_batch, tm):
    n, d = x.shape
    hw = DIL_HEADS * DIL_HEAD_DIM
    row = lambda width: pl.BlockSpec((tm, width), lambda i: (i, 0))
    return pl.pallas_call(
        _dil_combine_kernel,
        out_shape=jax.ShapeDtypeStruct((n, d), F32),
        grid=(n // tm,),
        in_specs=[row(d)] + [row(hw)] * 3 + [row(LANES)] * 3 + [pl.BlockSpec((hw, d), lambda i: (0, 0)),
                                                                 _mod_specs(per_row, tm, d, rows_per_batch)],
        out_specs=row(d),
        compiler_params=_cparams(("arbitrary",)),
        name="dil_combine",
    )(x, *nums, *stats, w, gate)


PEER_CNT = tuple(PEER_TOPK // (a + 1) for a in range(PEER_TOPK))
NEG_INF = float("-inf")


def _peer_route_kernel(x_ref, sh_ref, sc_ref, wq_ref, sk_ref,
                       hb_ref, g_ref, i_ref, j_ref,
                       st_ref, sv_ref, si_ref):
    k = PEER_TOPK
    nk = PEER_NKEYS
    x = x_ref[...]
    tm = x.shape[0]
    nchunk = tm // LANES
    h = _rms(x) * (1.0 + sc_ref[...]) + sh_ref[...]
    hb = h.astype(BF16)
    hb_ref[...] = hb
    q = _dot(hb, wq_ref[...]).astype(BF16)
    st = _dot_nt(sk_ref[...], q)
    for c in range(nchunk):
        st_ref[c] = st[:, c * LANES:(c + 1) * LANES]
    rk = lax.broadcasted_iota(I32, (nk, LANES), 0)
    r16 = lax.broadcasted_iota(I32, (k, LANES), 0)
    r8 = lax.broadcasted_iota(I32, (8, LANES), 0)

    def top_keys(c, hp):
        s = st_ref[c, pl.ds(pl.multiple_of(hp * nk, nk), nk), :]
        sv = jnp.zeros((k, LANES), F32)
        si = jnp.zeros((k, LANES), I32)
        for it in range(k):
            m = jnp.max(s, axis=0, keepdims=True)
            idx = jnp.min(jnp.where(s == m, rk, nk), axis=0, keepdims=True)
            s = jnp.where(rk == idx, NEG_INF, s)
            sv = jnp.where(r16 == it, m, sv)
            si = jnp.where(r16 == it, idx, si)
        sv_ref[c, pl.ds(pl.multiple_of(hp * k, k), k), :] = sv
        si_ref[c, pl.ds(pl.multiple_of(hp * k, k), k), :] = si

    def stage1(t, _):
        c = t // PEER_HEADS
        hh = t % PEER_HEADS
        top_keys(c, 2 * hh)
        top_keys(c, 2 * hh + 1)
        return 0

    lax.fori_loop(0, nchunk * PEER_HEADS, stage1, 0)

    def top_pairs(c, hh):
        o0 = pl.multiple_of(hh * 2 * k, 2 * k)
        o1 = pl.multiple_of(hh * 2 * k + k, k)
        sv0 = sv_ref[c, pl.ds(o0, k), :]
        sv1 = sv_ref[c, pl.ds(o1, k), :]
        si0 = si_ref[c, pl.ds(o0, k), :]
        si1 = si_ref[c, pl.ds(o1, k), :]
        tiles, codes = [], []
        for a in range(8):
            for b0 in range(0, PEER_CNT[a], 8):
                tile = sv0[a:a + 1, :] + sv1[b0:b0 + 8, :]
                tiles.append(jnp.where(r8 + b0 < PEER_CNT[a], tile, NEG_INF))
                codes.append(a * k + b0 + r8)
        tiles.append(sv0[8:16, :] + sv1[0:1, :])
        codes.append((8 + r8) * k)
        pool = jnp.concatenate(tiles, axis=0)
        code = jnp.concatenate(codes, axis=0)
        ts = jnp.zeros((k, LANES), F32)
        ii = jnp.zeros((k, LANES), I32)
        jj = jnp.zeros((k, LANES), I32)
        for it in range(k):
            m = jnp.max(pool, axis=0, keepdims=True)
            sel = jnp.min(jnp.where(pool == m, code, k * k), axis=0, keepdims=True)
            pool = jnp.where(code == sel, NEG_INF, pool)
            iv = jnp.sum(jnp.where(r16 == (sel >> 4), si0, 0), axis=0, keepdims=True)
            jv = jnp.sum(jnp.where(r16 == (sel & (k - 1)), si1, 0), axis=0, keepdims=True)
            ts = jnp.where(r16 == it, m, ts)
            ii = jnp.where(r16 == it, iv, ii)
            jj = jnp.where(r16 == it, jv, jj)
        e = jnp.exp(ts - jnp.max(ts, axis=0, keepdims=True))
        dst = pl.ds(pl.multiple_of(hh * k, k), k)
        g_ref[c, dst, :] = e / jnp.sum(e, axis=0, keepdims=True)
        i_ref[c, dst, :] = ii
        j_ref[c, dst, :] = jj

    def stage2(t, _):
        c = t // (PEER_HEADS // 2)
        hh = (t % (PEER_HEADS // 2)) * 2
        top_pairs(c, hh)
        top_pairs(c, hh + 1)
        return 0

    lax.fori_loop(0, nchunk * (PEER_HEADS // 2), stage2, 0)


def peer_route(x, shift, scale, wq, sk, *, per_row, rows_per_batch, tm):
    n, d = x.shape
    nchunk = tm // LANES
    npick = PEER_HEADS * PEER_TOPK
    nrow = sk.shape[0]
    pick = lambda dt: jax.ShapeDtypeStruct((n // LANES, npick, LANES), dt)
    pick_spec = pl.BlockSpec((nchunk, npick, LANES), lambda i: (i, 0, 0))
    outs = pl.pallas_call(
        _peer_route_kernel,
        out_shape=(jax.ShapeDtypeStruct((n, d), BF16), pick(F32), pick(I32), pick(I32)),
        grid=(n // tm,),
        in_specs=[pl.BlockSpec((tm, d), lambda i: (i, 0)),
                  _mod_specs(per_row, tm, d, rows_per_batch),
                  _mod_specs(per_row, tm, d, rows_per_batch),
                  pl.BlockSpec(wq.shape, lambda i: (0, 0)),
                  pl.BlockSpec(sk.shape, lambda i: (0, 0))],
        out_specs=(pl.BlockSpec((tm, d), lambda i: (i, 0)),) + (pick_spec,) * 3,
        scratch_shapes=[pltpu.VMEM((nchunk, nrow, LANES), F32),
                        pltpu.VMEM((nchunk, 2 * npick, LANES), F32),
                        pltpu.VMEM((nchunk, 2 * npick, LANES), I32)],
        compiler_params=_cparams(("arbitrary",)),
        name="peer_route",
    )(x, shift, scale, wq, sk)
    to_rows = lambda a: jnp.swapaxes(a, 1, 2).reshape(n, npick)
    return (outs[0],) + tuple(to_rows(a) for a in outs[1:])


def _peer_subkey_matrix(subkeys):
    h, p, nk, dk = subkeys.shape
    eye = jnp.eye(h * p, dtype=F32)
    m = eye[:, None, :, None] * subkeys.reshape(h * p, nk, 1, dk)
    return m.reshape(h * p * nk, h * p * dk).astype(BF16)


PEER_EB = 1024
PEER_RC = 128


def _gather_lanes(x, idx):
    return jnp.take_along_axis(x, idx, axis=1, mode="promise_in_bounds")


def _peer_up_kernel(hb_ref, u_ref, i_ref, j_ref, g_ref, c_ref, a_ref, acc_ref):
    s = pl.program_id(1)
    tm = hb_ref.shape[0]
    nslab = PEER_EB // LANES

    @pl.when(s == 0)
    def _():
        acc_ref[...] = jnp.zeros(acc_ref.shape, F32)

    a_ref[...] = _dot_nt(hb_ref[...], u_ref[...])

    def chunk(r, _):
        rows = pl.ds(pl.multiple_of(r * PEER_RC, PEER_RC), PEER_RC)
        ii = i_ref[rows, :]
        jj = j_ref[rows, :]
        acc = acc_ref[rows, :]
        for sl in range(nslab):
            got = _gather_lanes(a_ref[rows, sl * LANES:(sl + 1) * LANES], jj)
            acc = acc + jnp.where(ii == s * nslab + sl, got, 0.0)
        acc_ref[rows, :] = acc
        return 0

    lax.fori_loop(0, tm // PEER_RC, chunk, 0)

    @pl.when(s == pl.num_programs(1) - 1)
    def _():
        a = acc_ref[...]
        c_ref[...] = g_ref[...] * (0.5 * a * (1.0 + lax.erf(a * float(2.0 ** -0.5))))


def peer_up(hb, u, ii, jj, g, *, tm):
    n, d = hb.shape
    npick = ii.shape[1]
    pick_spec = pl.BlockSpec((tm, npick), lambda i, s: (i, 0))
    return pl.pallas_call(
        _peer_up_kernel,
        out_shape=jax.ShapeDtypeStruct((n, npick), F32),
        grid=(n // tm, u.shape[0] // PEER_EB),
        in_specs=[pl.BlockSpec((tm, d), lambda i, s: (i, 0)),
                  pl.BlockSpec((PEER_EB, d), lambda i, s: (s, 0)),
                  pick_spec, pick_spec, pick_spec],
        out_specs=pick_spec,
        scratch_shapes=[pltpu.VMEM((tm, PEER_EB), F32), pltpu.VMEM((tm, npick), F32)],
        compiler_params=_cparams(("arbitrary", "arbitrary")),
        name="peer_up",
    )(hb, u, ii, jj, g)


PEER_TG = 16
PEER_TU = 8


def _peer_down_kernel(x_ref, gate_ref, c_ref, i_ref, j_ref, v_ref, y_ref, wa_ref, stage_ref, acc_ref):
    s = pl.program_id(1)
    tm = x_ref.shape[0]
    nk = PEER_NKEYS
    nslab = PEER_EB // LANES

    @pl.when(s == 0)
    def _():
        acc_ref[...] = jnp.zeros(acc_ref.shape, F32)
        key = lax.broadcasted_iota(I32, (nk, LANES), 0)

        def group(gi, _):
            t0 = pl.multiple_of(gi * PEER_TG, PEER_TG)

            def tokens(u, _):
                for tl in range(PEER_TU):
                    tt = u * PEER_TU + tl
                    row = pl.ds(t0 + tt, 1)
                    ei = jnp.where(key == i_ref[row, :], 1.0, 0.0).astype(BF16)
                    cj = jnp.where(key == j_ref[row, :], c_ref[row, :], 0.0).astype(BF16)
                    stage_ref[tt] = _dot_nt(ei, cj)
                return 0

            lax.fori_loop(0, PEER_TG // PEER_TU, tokens, 0)
            slabs = jnp.swapaxes(stage_ref[...], 0, 1).astype(BF16)
            for i in range(nk):
                wa_ref[i // nslab, pl.ds(t0, PEER_TG), (i % nslab) * LANES:(i % nslab + 1) * LANES] = slabs[i]
            return 0

        lax.fori_loop(0, tm // PEER_TG, group, 0)

    acc_ref[...] += _dot(wa_ref[s], v_ref[...])

    @pl.when(s == pl.num_programs(1) - 1)
    def _():
        y_ref[...] = x_ref[...] + gate_ref[...] * acc_ref[...]


def peer_down(x, gate, c, ii, jj, v, *, per_row, rows_per_batch, tm):
    n, d = x.shape
    npick = c.shape[1]
    nsteps = v.shape[0] // PEER_EB
    pick_spec = pl.BlockSpec((tm, npick), lambda i, s: (i, 0))
    gate_spec = (pl.BlockSpec((tm, d), lambda i, s: (i, 0)) if per_row else
                 pl.BlockSpec((None, 1, d), lambda i, s: ((i * tm) // rows_per_batch, 0, 0)))
    return pl.pallas_call(
        _peer_down_kernel,
        out_shape=jax.ShapeDtypeStruct((n, d), F32),
        grid=(n // tm, nsteps),
        in_specs=[pl.BlockSpec((tm, d), lambda i, s: (i, 0)), gate_spec,
                  pick_spec, pick_spec, pick_spec,
                  pl.BlockSpec((PEER_EB, d), lambda i, s: (s, 0))],
        out_specs=pl.BlockSpec((tm, d), lambda i, s: (i, 0)),
        scratch_shapes=[pltpu.VMEM((nsteps, tm, PEER_EB), BF16),
                        pltpu.VMEM((PEER_TG, PEER_NKEYS, LANES), F32),
                        pltpu.VMEM((tm, d), F32)],
        compiler_params=_cparams(("arbitrary", "arbitrary")),
        name="peer_down",
    )(x, gate, c, ii, jj, v)


def peer_sublayer(x, shift, scale, gate, wq, sk, u, v, *, per_row, rows_per_batch, tm_up, tm_down):
    hb, g, ii, jj = peer_route(x, shift, scale, wq, sk, per_row=per_row, rows_per_batch=rows_per_batch, tm=tm_up)
    c = peer_up(hb, u, ii, jj, g, tm=tm_up)
    return peer_down(x, gate, c, ii, jj, v, per_row=per_row, rows_per_batch=rows_per_batch, tm=tm_down)


TM_PROJ = 256
TM_PEER = 512
TM_PEER_DOWN = 512


def kernel(x_prompt, x_sample, c_prompt, c_sample, cache_mla, cache_dil0, cache_dil1, cache_dil2, page_table, a_mod_w, a_mod_b, a_w_dq, a_g_cq, a_w_uq, a_w_dkv, a_g_ckv, a_g_qn, a_g_qr, a_g_kr, a_w_uk, a_g_kn, a_w_uv, a_w_o, kv_mod_w, kv_mod_b, kv_w, kv_g_k, b_mod_w, b_mod_b, b_w_q, b_g_q, b_w_o, f_mod_w, f_mod_b, f_w_q, f_subkeys, f_u, f_v):
    b, s, d = x_prompt.shape
    db, t, _ = x_sample.shape
    n_p, n_s = b * s, db * t
    past = page_table.shape[1] * PAGE_SIZE
    depth = f_mod_w.shape[0]
    n_a = a_mod_w.shape[0]
    caches = (cache_dil0, cache_dil1, cache_dil2)
    hw = DIL_HEADS * DIL_HEAD_DIM
    lat_w = cache_mla.shape[-1]

    c_all = jnp.concatenate([c_prompt, c_sample], axis=0)
    xp = x_prompt.reshape(n_p, d)
    xs = x_sample.reshape(n_s, d)
    pos_p = jnp.arange(s, dtype=I32)
    pos_s = jnp.tile(past + jnp.arange(t, dtype=I32), db)
    mla_tab_p = _rope_tables(pos_p, MLA_SLAB_ROPE0, MLA_ROPE // 2)
    mla_tab_s = _rope_tables(pos_s, MLA_SLAB_ROPE0, MLA_ROPE // 2)
    dil_tab_p = _rope_tables(pos_p, 0, DIL_ROT // 2)
    dil_tab_s = _rope_tables(pos_s, 0, DIL_ROT // 2)
    kw_p = dict(per_row=False, rows_per_batch=s)
    kw_s = dict(per_row=True, rows_per_batch=t)

    def mods(w, bias):
        m = modulation(c_all, w, bias)
        ms = jnp.repeat(m[b:], t, axis=0)
        k = w.shape[1] // d
        return ([m[:b, None, i * d:(i + 1) * d] for i in range(k)], [ms[:, i * d:(i + 1) * d] for i in range(k)])

    rows_p, rows_s = [], []
    dil_p, dil_s = [], []
    kvf_p = kv_new = None
    for layer in range(depth):
        if layer < n_a:
            i = layer
            (sh_p, sc_p, gt_p), (sh_s, sc_s, gt_s) = mods(a_mod_w[i], a_mod_b[i])
            w = _mla_weights(a_w_dq[i], a_g_cq[i], a_w_uq[i], a_w_dkv[i], a_g_ckv[i], a_g_qn[i], a_g_qr[i],
                             a_g_kr[i], a_w_uk[i], a_g_kn[i], a_w_uv[i])
            wo = a_w_o[i].astype(BF16)
            q, k, v, ckv, kpe = mla_project(xp, sh_p, sc_p, *mla_tab_p, w, tm=TM_PROJ, **kw_p)
            rows_p.append(jnp.concatenate([ckv, kpe[:, MLA_NOPE:MLA_NOPE + MLA_ROPE]], axis=1).reshape(b, s, lat_w))
            o = mla_attention(q, k, v, b, s)
            xp = out_proj(xp, o, wo, gt_p, tm=TM_PROJ, **kw_p)
            q, k, v, ckv, kpe = mla_project(xs, sh_s, sc_s, *mla_tab_s, w, tm=n_s, **kw_s)
            rows_s.append(jnp.concatenate([ckv, kpe[:, MLA_NOPE:MLA_NOPE + MLA_ROPE]], axis=1).reshape(db, t, lat_w))
            o = mla_sample_attention(q, ckv, kpe, cache_mla[i], page_table, a_w_uk[i], a_g_kn[i], a_w_uv[i], t)
            xs = out_proj(xs, o, wo, gt_s, tm=n_s, **kw_s)
        else:
            if layer == n_a:
                (sh_p, sc_p), (sh_s, sc_s) = mods(kv_mod_w, kv_mod_b)
                kvw = kv_w.reshape(d, 2, N_GROUPS, hw)
                wkv = jnp.stack([jnp.concatenate([kvw[:, 0, g], kvw[:, 1, g]], axis=1) for g in range(N_GROUPS)])
                wkv = wkv.astype(BF16)
                gk = kv_g_k[:, None, :]
                kvf_p = ada_heads(xp, sh_p, sc_p, *dil_tab_p, wkv, gk, n_norm=DIL_HEADS, oscale=1.0,
                                  tm=TM_PROJ, **kw_p)
                kvf_s = ada_heads(xs, sh_s, sc_s, *dil_tab_s, wkv, gk, n_norm=DIL_HEADS, oscale=1.0,
                                  tm=n_s, **kw_s)
                new_p = kvf_p.reshape(b, s, N_GROUPS, 2, DIL_HEADS, DIL_HEAD_DIM)
                new_s = kvf_s.reshape(db, t, N_GROUPS, 2, DIL_HEADS, DIL_HEAD_DIM)
                for g in range(N_GROUPS):
                    win = DIL_WINDOWS[g]
                    full_s = jnp.concatenate([caches[g], new_s[:, :, g]], axis=1)
                    dil_p.append(new_p[:, s - min(win, s):, g])
                    dil_s.append(full_s[:, full_s.shape[1] - min(win, past + t):])
                kv_new = jnp.pad(kvf_s.reshape(db, t, -1), ((0, 0), (0, NEW_ROWS - t), (0, 0)))
            j = layer - n_a
            (sh_p, sc_p, gt_p), (sh_s, sc_s, gt_s) = mods(b_mod_w[j], b_mod_b[j])
            wq = jnp.transpose(b_w_q[j].reshape(d, N_GROUPS, hw), (1, 0, 2)).astype(BF16)
            gq = b_g_q[j][:, None, :]
            wo = b_w_o[j].astype(BF16)
            qscale = float(DIL_HEAD_DIM ** -0.5)
            qf = ada_heads(xp, sh_p, sc_p, *dil_tab_p, wq, gq, n_norm=DIL_HEADS, oscale=qscale, tm=TM_PROJ, **kw_p)
            parts = [dil_prompt_attention(qf, kvf_p, g, b, s) for g in range(N_GROUPS)]
            xp = dil_combine(xp, [p[0] for p in parts], [p[1] for p in parts], wo, gt_p, tm=TM_PROJ, **kw_p)
            qf = ada_heads(xs, sh_s, sc_s, *dil_tab_s, wq, gq, n_norm=DIL_HEADS, oscale=qscale, tm=n_s, **kw_s)
            parts = [dil_sample_attention(qf, caches[g], kv_new, g, t) for g in range(N_GROUPS)]
            xs = dil_combine(xs, [p[0] for p in parts], [p[1] for p in parts], wo, gt_s, tm=n_s, **kw_s)
        (sh_p, sc_p, gt_p), (sh_s, sc_s, gt_s) = mods(f_mod_w[layer], f_mod_b[layer])
        wq = f_w_q[layer].astype(BF16)
        sk = _peer_subkey_matrix(f_subkeys[layer])
        u = f_u[layer].astype(BF16)
        v = f_v[layer].astype(BF16)
        xp = peer_sublayer(xp, sh_p, sc_p, gt_p, wq, sk, u, v, tm_up=TM_PEER, tm_down=TM_PEER_DOWN, **kw_p)
        xs = peer_sublayer(xs, sh_s, sc_s, gt_s, wq, sk, u, v, tm_up=n_s, tm_down=n_s, **kw_s)
    return (xp.reshape(b, s, d), xs.reshape(db, t, d), jnp.stack(rows_p), jnp.stack(rows_s),
            dil_p[0], dil_s[0], dil_p[1], dil_s[1], dil_p[2], dil_s[2])
```

```python
import functools

import numpy as np
import jax
import jax.numpy as jnp
from jax import lax
from jax.experimental import pallas as pl
from jax.experimental.pallas import tpu as pltpu

F32 = jnp.float32
BF16 = jnp.bfloat16
I32 = jnp.int32

LANES = 128
VMEM_LIMIT = 56 * 1024 * 1024

ROPE_THETA = 500000.0
EPS = 1e-6
NEG = -1e30

MLA_HEADS = 16
MLA_NOPE = 64
MLA_ROPE = 32
MLA_V = 64
MLA_KV_LORA = 256
PAGE_SIZE = 128

DIL_WINDOWS = (128, 512, 2048)
DIL_RATES = (1, 4, 16)
N_GROUPS = 3
DIL_HEADS = 8
DIL_HEAD_DIM = 128
DIL_ROT = DIL_HEAD_DIM // 4

PEER_HEADS = 8
PEER_NKEYS = 128
PEER_TOPK = 16
PEER_DKEY = 128


def _cparams(sem):
    return pltpu.CompilerParams(dimension_semantics=sem, vmem_limit_bytes=VMEM_LIMIT)


def _rms(x):
    return x * lax.rsqrt(jnp.mean(x * x, axis=-1, keepdims=True) + EPS)


def _dot(a, b):
    return jnp.dot(a, b, preferred_element_type=F32)


def _dot_nt(a, b):
    return lax.dot_general(a, b, (((1,), (1,)), ((), ())), preferred_element_type=F32)


def _mod_specs(per_row, tm, d, rows_per_batch):
    if per_row:
        return pl.BlockSpec((tm, d), lambda i: (i, 0))
    return pl.BlockSpec((None, 1, d), lambda i: ((i * tm) // rows_per_batch, 0, 0))


def _mod_kernel(c_ref, w_ref, b_ref, o_ref):
    c = c_ref[...]
    s = (c * jax.nn.sigmoid(c)).astype(BF16)
    o_ref[...] = _dot(s, w_ref[...].astype(BF16)) + b_ref[...]


def modulation(c, w, b):
    bc, d = c.shape
    n = w.shape[1]
    tn = 1024
    return pl.pallas_call(
        _mod_kernel,
        out_shape=jax.ShapeDtypeStruct((bc, n), F32),
        grid=(n // tn,),
        in_specs=[pl.BlockSpec((bc, d), lambda j: (0, 0)),
                  pl.BlockSpec((d, tn), lambda j: (0, j)),
                  pl.BlockSpec((1, tn), lambda j: (0, j))],
        out_specs=pl.BlockSpec((bc, tn), lambda j: (0, j)),
        compiler_params=_cparams(("arbitrary",)),
        name="modulation",
    )(c, w, b.reshape(1, n))


def _rope_tables(pos, lane0, half):
    inv = ROPE_THETA ** (-jnp.arange(half, dtype=F32) / half)
    ang = pos.astype(F32)[:, None] * inv
    cos, sin = jnp.cos(ang), jnp.sin(ang)
    t = pos.shape[0]
    cos_t = jnp.ones((t, LANES), F32)
    sin_t = jnp.zeros((t, LANES), F32)
    cos_t = cos_t.at[:, lane0:lane0 + half].set(cos).at[:, lane0 + half:lane0 + 2 * half].set(cos)
    sin_t = sin_t.at[:, lane0:lane0 + half].set(-sin).at[:, lane0 + half:lane0 + 2 * half].set(sin)
    return cos_t, sin_t


def _rope_slab(s, cos_t, sin_t, lane, lane0, half):
    swapped = jnp.where(lane < lane0 + half, pltpu.roll(s, LANES - half, 1), pltpu.roll(s, half, 1))
    return s * cos_t + swapped * sin_t


MLA_SLAB_ROPE0 = MLA_NOPE


def _mla_proj_kernel(x_ref, sh_ref, sc_ref, cos_ref, sin_ref, wdq_ref, gcq_ref, wuq_ref, wdkv_ref, gckv_ref,
                     gq_ref, gkr_ref, wuk_ref, gkn_ref, wuv_ref,
                     q_ref, k_ref, v_ref, ckv_ref, kpe_ref, *, qscale):
    x = x_ref[...]
    tm = x.shape[0]
    h = _rms(x) * (1.0 + sc_ref[...]) + sh_ref[...]
    hb = h.astype(BF16)
    cq = _rms(_dot(hb, wdq_ref[...])) * gcq_ref[...]
    q = _dot(cq.astype(BF16), wuq_ref[...])
    kv = _dot(hb, wdkv_ref[...])
    ckv = _rms(kv[:, :MLA_KV_LORA]) * gckv_ref[...]
    cos_t, sin_t = cos_ref[...], sin_ref[...]
    lane = lax.broadcasted_iota(I32, (tm, LANES), 1)
    m_n = lane < MLA_NOPE
    m_r = (lane >= MLA_NOPE) & (lane < MLA_NOPE + MLA_ROPE)

    def norm_rope(s, g):
        s2 = s * s
        ssn = jnp.sum(jnp.where(m_n, s2, 0.0), axis=-1, keepdims=True)
        ssr = jnp.sum(jnp.where(m_r, s2, 0.0), axis=-1, keepdims=True)
        r = jnp.where(m_n, lax.rsqrt(ssn / MLA_NOPE + EPS), lax.rsqrt(ssr / MLA_ROPE + EPS))
        return _rope_slab(s * r * g, cos_t, sin_t, lane, MLA_SLAB_ROPE0, MLA_ROPE // 2)

    kpe = norm_rope(kv[:, MLA_KV_LORA:], gkr_ref[...])
    ckv_ref[...] = ckv
    kpe_ref[...] = kpe
    ckvb = ckv.astype(BF16)
    kn = _dot(ckvb, wuk_ref[...])
    v_ref[...] = _dot(ckvb, wuv_ref[...]).astype(BF16)
    gq = gq_ref[...]
    gkn = gkn_ref[...]
    for hh in range(MLA_HEADS):
        sl = slice(hh * LANES, (hh + 1) * LANES)
        q_ref[hh] = (norm_rope(q[:, sl], gq) * qscale).astype(BF16)
        ks = kn[:, sl]
        ssn = jnp.sum(ks * ks, axis=-1, keepdims=True)
        k_ref[hh] = (ks * lax.rsqrt(ssn / MLA_NOPE + EPS) * gkn + kpe).astype(BF16)


def _mla_weights(w_dq, g_cq, w_uq, w_dkv, g_ckv, g_qn, g_qr, g_kr, w_uk, g_kn, w_uv):
    d = w_dq.shape[0]
    hd = MLA_NOPE + MLA_ROPE
    pad = LANES - hd
    wuq = jnp.pad(w_uq.reshape(-1, MLA_HEADS, hd), ((0, 0), (0, 0), (0, pad))).reshape(-1, MLA_HEADS * LANES)
    wdkv = jnp.concatenate([w_dkv[:, :MLA_KV_LORA], jnp.zeros((d, MLA_NOPE), F32), w_dkv[:, MLA_KV_LORA:],
                            jnp.zeros((d, pad), F32)], axis=1)
    gq = jnp.concatenate([g_qn, g_qr, jnp.zeros((pad,), F32)])[None]
    gkr = jnp.concatenate([jnp.zeros((MLA_NOPE,), F32), g_kr, jnp.zeros((pad,), F32)])[None]
    wuk = jnp.pad(w_uk, ((0, 0), (0, 0), (0, LANES - MLA_NOPE))).reshape(MLA_KV_LORA, MLA_HEADS * LANES)
    gkn = jnp.concatenate([g_kn, jnp.zeros((LANES - MLA_NOPE,), F32)])[None]
    wuv = w_uv.reshape(MLA_KV_LORA, MLA_HEADS * MLA_V)
    return dict(wdq=w_dq.astype(BF16), gcq=g_cq[None], wuq=wuq.astype(BF16), wdkv=wdkv.astype(BF16),
                gckv=g_ckv[None], gq=gq, gkr=gkr, wuk=wuk.astype(BF16), gkn=gkn, wuv=wuv.astype(BF16))


def mla_project(x, shift, scale, cos_t, sin_t, w, *, per_row, rows_per_batch, tm):
    n, d = x.shape
    nblk_pos = cos_t.shape[0] // tm
    full = lambda a: pl.BlockSpec(a.shape, lambda i: (0,) * a.ndim)
    wnames = ("wdq", "gcq", "wuq", "wdkv", "gckv", "gq", "gkr", "wuk", "gkn", "wuv")
    qscale = float((MLA_NOPE + MLA_ROPE) ** -0.5)
    return pl.pallas_call(
        functools.partial(_mla_proj_kernel, qscale=qscale),
        out_shape=(jax.ShapeDtypeStruct((MLA_HEADS, n, LANES), BF16),
                   jax.ShapeDtypeStruct((MLA_HEADS, n, LANES), BF16),
                   jax.ShapeDtypeStruct((n, MLA_HEADS * MLA_V), BF16),
                   jax.ShapeDtypeStruct((n, MLA_KV_LORA), F32),
                   jax.ShapeDtypeStruct((n, LANES), F32)),
        grid=(n // tm,),
        in_specs=[pl.BlockSpec((tm, d), lambda i: (i, 0)),
                  _mod_specs(per_row, tm, d, rows_per_batch),
                  _mod_specs(per_row, tm, d, rows_per_batch),
                  pl.BlockSpec((tm, LANES), lambda i: (i % nblk_pos, 0)),
                  pl.BlockSpec((tm, LANES), lambda i: (i % nblk_pos, 0))] + [full(w[k]) for k in wnames],
        out_specs=(pl.BlockSpec((MLA_HEADS, tm, LANES), lambda i: (0, i, 0)),
                   pl.BlockSpec((MLA_HEADS, tm, LANES), lambda i: (0, i, 0)),
                   pl.BlockSpec((tm, MLA_HEADS * MLA_V), lambda i: (i, 0)),
                   pl.BlockSpec((tm, MLA_KV_LORA), lambda i: (i, 0)),
                   pl.BlockSpec((tm, LANES), lambda i: (i, 0))),
        compiler_params=_cparams(("arbitrary",)),
        name="mla_project",
    )(x, shift, scale, cos_t, sin_t, *[w[k] for k in wnames])


def _mla_attn_kernel(q_ref, k_ref, v_ref, o_ref, *, tq):
    qi = pl.program_id(2)
    row = lax.broadcasted_iota(I32, (tq, tq), 0)
    col = lax.broadcasted_iota(I32, (tq, tq), 1)
    nh = q_ref.shape[0]
    hs = range(nh)
    qs = [q_ref[hh] for hh in hs]

    def step(j, carry, masked):
        off = pl.multiple_of(j * tq, tq)
        vcs = [v_ref[pl.ds(off, tq), hp * LANES:(hp + 1) * LANES] for hp in range(nh // 2)]
        ss = [_dot_nt(qs[hh], k_ref[hh, pl.ds(off, tq), :]) for hh in hs]
        if masked:
            ss = [jnp.where(col <= row, s, NEG) for s in ss]
        m_new = [jnp.maximum(carry[hh][0], jnp.max(ss[hh], axis=-1, keepdims=True)) for hh in hs]
        alpha = [jnp.exp(carry[hh][0] - m_new[hh]) for hh in hs]
        ps = [jnp.exp(ss[hh] - m_new[hh]) for hh in hs]
        ls = [alpha[hh] * carry[hh][1] + jnp.sum(ps[hh], axis=-1, keepdims=True) for hh in hs]
        pv = [_dot(ps[hh].astype(BF16), vcs[hh // 2]) for hh in hs]
        return tuple((m_new[hh], ls[hh], alpha[hh] * carry[hh][2] + pv[hh]) for hh in hs)

    init = (jnp.full((tq, 1), NEG, F32), jnp.zeros((tq, 1), F32), jnp.zeros((tq, LANES), F32))
    carry = lax.fori_loop(0, qi, functools.partial(step, masked=False), (init,) * nh)
    carry = step(qi, carry, masked=True)
    lane = lax.broadcasted_iota(I32, (tq, LANES), 1)
    for hp in range(nh // 2):
        even, odd = carry[2 * hp], carry[2 * hp + 1]
        o_ref[:, hp * LANES:(hp + 1) * LANES] = jnp.where(lane < MLA_V, even[2] / even[1], odd[2] / odd[1]).astype(BF16)


MLA_ATTN_HEADS = 4


def mla_attention(q, k, v, b, s):
    tq = 256
    nh = MLA_ATTN_HEADS
    vw = nh * MLA_V
    q4 = q.reshape(MLA_HEADS, b, s, LANES)
    k4 = k.reshape(MLA_HEADS, b, s, LANES)
    v3 = v.reshape(b, s, MLA_HEADS * MLA_V)
    out = pl.pallas_call(
        functools.partial(_mla_attn_kernel, tq=tq),
        out_shape=jax.ShapeDtypeStruct((b, s, MLA_HEADS * MLA_V), BF16),
        grid=(b, MLA_HEADS // nh, s // tq),
        in_specs=[pl.BlockSpec((nh, None, tq, LANES), lambda bi, h, qi: (h, bi, qi, 0)),
                  pl.BlockSpec((nh, None, s, LANES), lambda bi, h, qi: (h, bi, 0, 0)),
                  pl.BlockSpec((None, s, vw), lambda bi, h, qi: (bi, 0, h))],
        out_specs=pl.BlockSpec((None, tq, vw), lambda bi, h, qi: (bi, qi, h)),
        compiler_params=_cparams(("arbitrary", "arbitrary", "arbitrary")),
        name="mla_attention",
    )(q4, k4, v3)
    return out.reshape(b * s, MLA_HEADS * MLA_V)


def _out_proj_kernel(x_ref, o_ref, w_ref, g_ref, y_ref):
    y_ref[...] = x_ref[...] + g_ref[...] * _dot(o_ref[...], w_ref[...])


def out_proj(x, o, w, gate, *, per_row, rows_per_batch, tm):
    n, d = x.shape
    kdim = o.shape[1]
    return pl.pallas_call(
        _out_proj_kernel,
        out_shape=jax.ShapeDtypeStruct((n, d), F32),
        grid=(n // tm,),
        in_specs=[pl.BlockSpec((tm, d), lambda i: (i, 0)),
                  pl.BlockSpec((tm, kdim), lambda i: (i, 0)),
                  pl.BlockSpec((kdim, d), lambda i: (0, 0)),
                  _mod_specs(per_row, tm, d, rows_per_batch)],
        out_specs=pl.BlockSpec((tm, d), lambda i: (i, 0)),
        compiler_params=_cparams(("arbitrary",)),
        name="out_proj",
    )(x, o, w, gate)


MLA_DEC_PAGES = 8
NEW_ROWS = 8


def _qabs_kernel(q_ref, w_ref, o_ref):
    o_ref[...] = _dot(q_ref[...], w_ref[...]).astype(BF16)


def mla_absorb_q(q, w_uk, g_kn):
    h, n, _ = q.shape
    wt = jnp.transpose(w_uk, (1, 2, 0)) * g_kn[None, :, None]
    wt = jnp.pad(wt, ((0, 0), (0, LANES - MLA_NOPE), (0, 0))).astype(BF16)
    return pl.pallas_call(
        _qabs_kernel,
        out_shape=jax.ShapeDtypeStruct((h, n, MLA_KV_LORA), BF16),
        grid=(h,),
        in_specs=[pl.BlockSpec((None, n, LANES), lambda i: (i, 0, 0)),
                  pl.BlockSpec((None, LANES, MLA_KV_LORA), lambda i: (i, 0, 0))],
        out_specs=pl.BlockSpec((None, n, MLA_KV_LORA), lambda i: (i, 0, 0)),
        compiler_params=_cparams(("arbitrary",)),
        name="mla_absorb_q",
    )(q, wt)


def _mla_decode_kernel(pt_ref, qa_ref, qp_ref, new_ref, wukt_ref, *rest, n_dec):
    page_refs = rest[:MLA_DEC_PAGES]
    o_ref = rest[MLA_DEC_PAGES]
    m_ref, l_ref, acc_ref = rest[MLA_DEC_PAGES + 1:]
    j = pl.program_id(1)
    qa = qa_ref[...]
    qp = qp_ref[...]
    nrow = qa.shape[0]

    def scores(lat_t):
        n = lat_t.shape[1]
        ckvt = lat_t[:MLA_KV_LORA].astype(BF16)
        kpet = lat_t[MLA_KV_LORA:].astype(BF16)
        kn = _dot(wukt_ref[...], ckvt)
        ssq = jnp.sum((kn * kn).reshape(MLA_HEADS, MLA_NOPE, n), axis=1)
        r = lax.rsqrt(ssq / MLA_NOPE + EPS)
        rr = jnp.concatenate([r] * n_dec + [jnp.zeros((nrow - n_dec * MLA_HEADS, n), F32)], axis=0)
        return _dot(qa, ckvt) * rr + _dot(qp, kpet), ckvt

    def accumulate(s, ckvt):
        m = m_ref[...]
        m_new = jnp.maximum(m, jnp.max(s, axis=-1, keepdims=True))
        alpha = jnp.exp(m - m_new)
        p = jnp.exp(s - m_new)
        l_ref[...] = alpha * l_ref[...] + jnp.sum(p, axis=-1, keepdims=True)
        acc_ref[...] = alpha * acc_ref[...] + _dot_nt(p.astype(BF16), ckvt)
        m_ref[...] = m_new

    @pl.when(j == 0)
    def _():
        m_ref[...] = jnp.full(m_ref.shape, NEG, F32)
        l_ref[...] = jnp.zeros(l_ref.shape, F32)
        acc_ref[...] = jnp.zeros(acc_ref.shape, F32)
        s, ckvt = scores(new_ref[...])
        key = lax.broadcasted_iota(I32, s.shape, 1)
        tok = lax.broadcasted_iota(I32, s.shape, 0) // MLA_HEADS
        accumulate(jnp.where((key <= tok) & (key < n_dec), s, NEG), ckvt)

    s, ckvt = scores(jnp.concatenate([r[...] for r in page_refs], axis=1))
    accumulate(s, ckvt)

    @pl.when(j == pl.num_programs(1) - 1)
    def _():
        o_ref[...] = acc_ref[...] / l_ref[...]


def mla_decode_attention(qabs, qpe, lat_new_t, pool_t, page_table, w_uk, n_dec):
    db, n_pages = page_table.shape
    lat = pool_t.shape[1]
    nrow = qabs.shape[1]
    wukt = w_uk.reshape(MLA_KV_LORA, MLA_HEADS * MLA_NOPE).T.astype(BF16)
    steps = n_pages // MLA_DEC_PAGES
    page_specs = [pl.BlockSpec((None, lat, PAGE_SIZE),
                               lambda b, j, pt, pp=pp: (pt[b * n_pages + j * MLA_DEC_PAGES + pp], 0, 0))
                  for pp in range(MLA_DEC_PAGES)]
    grid_spec = pltpu.PrefetchScalarGridSpec(
        num_scalar_prefetch=1,
        grid=(db, steps),
        in_specs=[pl.BlockSpec((None, nrow, MLA_KV_LORA), lambda b, j, pt: (b, 0, 0)),
                  pl.BlockSpec((None, nrow, MLA_ROPE), lambda b, j, pt: (b, 0, 0)),
                  pl.BlockSpec((None, lat, LANES), lambda b, j, pt: (b, 0, 0)),
                  pl.BlockSpec(wukt.shape, lambda b, j, pt: (0, 0))] + page_specs,
        out_specs=pl.BlockSpec((None, nrow, MLA_KV_LORA), lambda b, j, pt: (b, 0, 0)),
        scratch_shapes=[pltpu.VMEM((nrow, 1), F32), pltpu.VMEM((nrow, 1), F32),
                        pltpu.VMEM((nrow, MLA_KV_LORA), F32)],
    )
    return pl.pallas_call(
        functools.partial(_mla_decode_kernel, n_dec=n_dec),
        out_shape=jax.ShapeDtypeStruct((db, nrow, MLA_KV_LORA), F32),
        grid_spec=grid_spec,
        compiler_params=_cparams(("arbitrary", "arbitrary")),
        name="mla_decode_attention",
    )(page_table.reshape(-1), qabs, qpe, lat_new_t, wukt, *([pool_t] * MLA_DEC_PAGES))


def _mla_uv_kernel(x_ref, w_ref, o_ref):
    w = w_ref[...]
    lane = lax.broadcasted_iota(I32, o_ref.shape, 1)
    o_ref[...] = jnp.where(lane < MLA_V, _dot(x_ref[0], w), _dot(x_ref[1], w)).astype(BF16)


def mla_decode_values(x, w_uv):
    h, n, _ = x.shape
    wuv = w_uv.reshape(MLA_KV_LORA, MLA_HEADS * MLA_V).astype(BF16)
    return pl.pallas_call(
        _mla_uv_kernel,
        out_shape=jax.ShapeDtypeStruct((n, MLA_HEADS * MLA_V), BF16),
        grid=(h // 2,),
        in_specs=[pl.BlockSpec((2, n, MLA_KV_LORA), lambda i: (i, 0, 0)),
                  pl.BlockSpec((MLA_KV_LORA, LANES), lambda i: (0, i))],
        out_specs=pl.BlockSpec((n, LANES), lambda i: (0, i)),
        compiler_params=_cparams(("arbitrary",)),
        name="mla_decode_values",
    )(x, wuv)


def mla_sample_attention(q, ckv, kpe, pool, page_table, w_uk, g_kn, w_uv, n_dec):
    db = page_table.shape[0]
    n = db * n_dec
    qabs = mla_absorb_q(q, w_uk, g_kn)
    nq = MLA_HEADS * n_dec
    assert nq <= LANES

    def to_rows(a):
        f = a.shape[-1]
        a = a.reshape(MLA_HEADS, db, n_dec, f).transpose(1, 2, 0, 3).reshape(db, nq, f)
        return jnp.pad(a, ((0, 0), (0, LANES - nq), (0, 0)))

    lat_new = jnp.concatenate([ckv, kpe[:, MLA_NOPE:MLA_NOPE + MLA_ROPE]], axis=1).reshape(db, n_dec, -1)
    lat_new_t = jnp.pad(jnp.swapaxes(lat_new, 1, 2), ((0, 0), (0, 0), (0, LANES - n_dec)))
    out = mla_decode_attention(to_rows(qabs), to_rows(q[:, :, MLA_NOPE:MLA_NOPE + MLA_ROPE]), lat_new_t,
                               jnp.swapaxes(pool, 1, 2), page_table, w_uk, n_dec)
    x = out[:, :nq].reshape(db, n_dec, MLA_HEADS, MLA_KV_LORA).transpose(2, 0, 1, 3)
    x = x.reshape(MLA_HEADS, n, MLA_KV_LORA).astype(BF16)
    return mla_decode_values(x, w_uv)


def _ada_heads_kernel(x_ref, sh_ref, sc_ref, cos_ref, sin_ref, w_ref, g_ref, o_ref, *, n_norm, n_slab, oscale):
    x = x_ref[...]
    tm = x.shape[0]
    h = _rms(x) * (1.0 + sc_ref[...]) + sh_ref[...]
    y = _dot(h.astype(BF16), w_ref[...])
    cos_t, sin_t = cos_ref[...], sin_ref[...]
    g = g_ref[...]
    lane = lax.broadcasted_iota(I32, (tm, LANES), 1)
    for s in range(n_slab):
        sl = slice(s * LANES, (s + 1) * LANES)
        ys = y[:, sl]
        if s < n_norm:
            ys = _rope_slab(_rms(ys) * g, cos_t, sin_t, lane, 0, DIL_ROT // 2) * oscale
        o_ref[:, sl] = ys


def ada_heads(x, shift, scale, cos_t, sin_t, w, g, *, n_norm, oscale, per_row, rows_per_batch, tm):
    n, d = x.shape
    ng, _, gw = w.shape
    n_slab = gw // LANES
    nblk_pos = cos_t.shape[0] // tm
    mod_spec = (pl.BlockSpec((tm, d), lambda gi, i: (i, 0)) if per_row else
                pl.BlockSpec((None, 1, d), lambda gi, i: ((i * tm) // rows_per_batch, 0, 0)))
    return pl.pallas_call(
        functools.partial(_ada_heads_kernel, n_norm=n_norm, n_slab=n_slab, oscale=oscale),
        out_shape=jax.ShapeDtypeStruct((n, ng * gw), F32),
        grid=(ng, n // tm),
        in_specs=[pl.BlockSpec((tm, d), lambda gi, i: (i, 0)), mod_spec, mod_spec,
                  pl.BlockSpec((tm, LANES), lambda gi, i: (i % nblk_pos, 0)),
                  pl.BlockSpec((tm, LANES), lambda gi, i: (i % nblk_pos, 0)),
                  pl.BlockSpec((None, d, gw), lambda gi, i: (gi, 0, 0)),
                  pl.BlockSpec((None, 1, LANES), lambda gi, i: (gi, 0, 0))],
        out_specs=pl.BlockSpec((tm, gw), lambda gi, i: (i, gi)),
        compiler_params=_cparams(("arbitrary", "arbitrary")),
        name="ada_heads",
    )(x, shift, scale, cos_t, sin_t, w, g)


DIL_SPAN = 128
STAT_DEN0 = DIL_HEADS


def _dil_prompt_kernel(q_ref, kp_ref, vp_ref, kc_ref, vc_ref, num_ref, st_ref, *, dil, hps):
    nb = pl.program_id(1)
    hs = pl.program_id(2)
    span = DIL_SPAN
    iq = lax.broadcasted_iota(I32, (span, span), 0)
    jk = lax.broadcasted_iota(I32, (span, span), 1)
    mask_prev = (jk >= iq) & (nb > 0)
    mask_cur = jk <= iq
    lane = lax.broadcasted_iota(I32, (span, LANES), 1)

    @pl.when(hs == 0)
    def _():
        st_ref[...] = jnp.zeros(st_ref.shape, F32)

    def attend(items):
        n = range(len(items))
        qs = [q_ref[rows, sl].astype(BF16) for rows, sl, _ in items]
        sps = [jnp.where(mask_prev, _dot_nt(qs[i], kp_ref[items[i][0], items[i][1]].astype(BF16)), NEG) for i in n]
        scs = [jnp.where(mask_cur, _dot_nt(qs[i], kc_ref[items[i][0], items[i][1]].astype(BF16)), NEG) for i in n]
        ms = [jnp.maximum(jnp.max(sps[i], axis=-1, keepdims=True), jnp.max(scs[i], axis=-1, keepdims=True)) for i in n]
        pps = [jnp.exp(sps[i] - ms[i]) for i in n]
        pcs = [jnp.exp(scs[i] - ms[i]) for i in n]
        dens = [jnp.sum(pps[i], axis=-1, keepdims=True) + jnp.sum(pcs[i], axis=-1, keepdims=True) for i in n]
        for i in n:
            rows, sl, hh = items[i]
            num_ref[rows, sl] = (_dot(pps[i].astype(BF16), vp_ref[rows, sl].astype(BF16))
                                 + _dot(pcs[i].astype(BF16), vc_ref[rows, sl].astype(BF16)))
        for i in n:
            rows, _, hh = items[i]
            stats = st_ref[rows, :]
            stats = jnp.where(lane == hh, ms[i], stats)
            st_ref[rows, :] = jnp.where(lane == STAT_DEN0 + hh, dens[i], stats)

    if dil > 1:
        per_trip = 2 if dil < 8 else 4

        def residues(u, _):
            attend([(pl.ds(u * per_trip + k, span, stride=dil), slice(0, LANES), hs) for k in range(per_trip)])
            return 0

        lax.fori_loop(0, dil // per_trip, residues, 0)
    else:
        for h0 in range(0, hps, 4):
            attend([(pl.ds(0, span), slice(hl * LANES, (hl + 1) * LANES), hs * hps + hl) for hl in range(h0, h0 + 4)])


DIL_HEADS_PER_STEP = (8, 1, 1)


def dil_prompt_attention(q, kv, g, b, s):
    dil = DIL_RATES[g]
    span = DIL_WINDOWS[g] // dil
    assert span == DIL_SPAN and s % (dil * span) == 0
    rows = span * dil
    n_blk = s // rows
    hps = DIL_HEADS_PER_STEP[g]
    hw = DIL_HEADS * DIL_HEAD_DIM
    bw = hps * DIL_HEAD_DIM
    nh = DIL_HEADS // hps
    q3 = q.reshape(b, s, q.shape[1])
    kv3 = kv.reshape(b, s, kv.shape[1])
    prev = lambda nb: jnp.maximum(nb - 1, 0)
    blk = lambda row_of, col0: pl.BlockSpec((None, rows, bw), lambda bi, nb, hs: (bi, row_of(nb), col0 * nh + hs))
    cur = lambda nb: nb
    num, st = pl.pallas_call(
        functools.partial(_dil_prompt_kernel, dil=dil, hps=hps),
        out_shape=(jax.ShapeDtypeStruct((b, s, hw), F32), jax.ShapeDtypeStruct((b, s, LANES), F32)),
        grid=(b, n_blk, nh),
        in_specs=[blk(cur, g), blk(prev, 2 * g), blk(prev, 2 * g + 1), blk(cur, 2 * g), blk(cur, 2 * g + 1)],
        out_specs=(blk(cur, 0), pl.BlockSpec((None, rows, LANES), lambda bi, nb, hs: (bi, nb, 0))),
        compiler_params=_cparams(("arbitrary", "arbitrary", "arbitrary")),
        name=f"dil_prompt_attention_g{g}",
    )(q3, kv3, kv3, kv3, kv3)
    return num.reshape(b * s, hw), st.reshape(b * s, LANES)


def _dil_sample_kernel(q_ref, cache_ref, new_ref, num_ref, st_ref, *, dil, n_dec):
    t = pl.program_id(1)
    hw = DIL_HEADS * DIL_HEAD_DIM
    rows = cache_ref.shape[0]
    n_new = new_ref.shape[0]
    crow = lax.broadcasted_iota(I32, (rows, 1), 0)
    nrow = lax.broadcasted_iota(I32, (n_new, 1), 0)
    cache_ok = (crow >= t) if dil == 1 else (crow >= 0)
    new_ok = (nrow <= t) & (nrow < n_dec) if dil == 1 else (nrow == t)
    lane = lax.broadcasted_iota(I32, (1, LANES), 1)
    stats = jnp.zeros((1, LANES), F32)
    for hh in range(DIL_HEADS):
        sl = slice(hh * LANES, (hh + 1) * LANES)
        q = q_ref[:, sl]
        sc = jnp.where(cache_ok, jnp.sum(cache_ref[:, 0, hh, :] * q, axis=-1, keepdims=True), NEG)
        sn = jnp.where(new_ok, jnp.sum(new_ref[:, sl] * q, axis=-1, keepdims=True), NEG)
        m = jnp.maximum(jnp.max(sc, axis=0, keepdims=True), jnp.max(sn, axis=0, keepdims=True))
        pc = jnp.exp(sc - m)
        pn = jnp.exp(sn - m)
        den = jnp.sum(pc, axis=0, keepdims=True) + jnp.sum(pn, axis=0, keepdims=True)
        vs = slice(hw + hh * LANES, hw + (hh + 1) * LANES)
        num_ref[:, sl] = (jnp.sum(pc * cache_ref[:, 1, hh, :], axis=0, keepdims=True)
                          + jnp.sum(pn * new_ref[:, vs], axis=0, keepdims=True))
        stats = jnp.where(lane == hh, m, stats)
        stats = jnp.where(lane == STAT_DEN0 + hh, den, stats)
    st_ref[...] = stats


def dil_sample_attention(q, cache, kv_new, g, n_dec):
    dil = DIL_RATES[g]
    db, w = cache.shape[:2]
    assert w == DIL_WINDOWS[g] and w // dil == DIL_SPAN and (dil == 1 or n_dec <= dil)
    hw = DIL_HEADS * DIL_HEAD_DIM
    rows = w // dil
    cv = cache.reshape(db, rows, dil, 2, DIL_HEADS, DIL_HEAD_DIM)
    q4 = q.reshape(db, n_dec, 1, q.shape[1])
    n_new = kv_new.shape[1]
    num, st = pl.pallas_call(
        functools.partial(_dil_sample_kernel, dil=dil, n_dec=n_dec),
        out_shape=(jax.ShapeDtypeStruct((db, n_dec, 1, hw), F32),
                   jax.ShapeDtypeStruct((db, n_dec, 1, LANES), F32)),
        grid=(db, n_dec),
        in_specs=[pl.BlockSpec((None, None, 1, hw), lambda bi, t: (bi, t, 0, g)),
                  pl.BlockSpec((None, rows, None, 2, DIL_HEADS, DIL_HEAD_DIM),
                               lambda bi, t: (bi, 0, t if dil > 1 else 0, 0, 0, 0)),
                  pl.BlockSpec((None, n_new, 2 * hw), lambda bi, t: (bi, 0, g))],
        out_specs=(pl.BlockSpec((None, None, 1, hw), lambda bi, t: (bi, t, 0, 0)),
                   pl.BlockSpec((None, None, 1, LANES), lambda bi, t: (bi, t, 0, 0))),
        compiler_params=_cparams(("arbitrary", "arbitrary")),
        name=f"dil_sample_attention_g{g}",
    )(q4, cv, kv_new)
    return num.reshape(db * n_dec, hw), st.reshape(db * n_dec, LANES)


def _dil_combine_kernel(x_ref, n0_ref, n1_ref, n2_ref, s0_ref, s1_ref, s2_ref, w_ref, g_ref, y_ref):
    nums = (n0_ref, n1_ref, n2_ref)
    stats = (s0_ref[...], s1_ref[...], s2_ref[...])
    outs = []
    for hh in range(DIL_HEADS):
        sl = slice(hh * LANES, (hh + 1) * LANES)
        ms = [st[:, hh:hh + 1] for st in stats]
        ds = [st[:, STAT_DEN0 + hh:STAT_DEN0 + hh + 1] for st in stats]
        big = jnp.maximum(jnp.maximum(ms[0], ms[1]), ms[2])
        num = 0.0
        den = 0.0
        for gi in range(N_GROUPS):
            wgt = jnp.exp(ms[gi] - big)
            num = num + wgt * nums[gi][:, sl]
            den = den + wgt * ds[gi]
        outs.append((num / den).astype(BF16))
    o = jnp.concatenate(outs, axis=1)
    y_ref[...] = x_ref[...] + g_ref[...] * _dot(o, w_ref[...])


def dil_combine(x, nums, stats, w, gate, *, per_row, rows_per_batch, tm):
    n, d = x.shape
    hw = DIL_HEADS * DIL_HEAD_DIM
    row = lambda width: pl.BlockSpec((tm, width), lambda i: (i, 0))
    return pl.pallas_call(
        _dil_combine_kernel,
        out_shape=jax.ShapeDtypeStruct((n, d), F32),
        grid=(n // tm,),
        in_specs=[row(d)] + [row(hw)] * 3 + [row(LANES)] * 3 + [pl.BlockSpec((hw, d), lambda i: (0, 0)),
                                                                 _mod_specs(per_row, tm, d, rows_per_batch)],
        out_specs=row(d),
        compiler_params=_cparams(("arbitrary",)),
        name="dil_combine",
    )(x, *nums, *stats, w, gate)


PEER_CNT = tuple(PEER_TOPK // (a + 1) for a in range(PEER_TOPK))
NEG_INF = float("-inf")


def _peer_route_kernel(x_ref, sh_ref, sc_ref, wq_ref, sk_ref,
                       hb_ref, g_ref, i_ref, j_ref,
                       st_ref, sv_ref, si_ref):
    k = PEER_TOPK
    nk = PEER_NKEYS
    x = x_ref[...]
    tm = x.shape[0]
    nchunk = tm // LANES
    h = _rms(x) * (1.0 + sc_ref[...]) + sh_ref[...]
    hb = h.astype(BF16)
    hb_ref[...] = hb
    q = _dot(hb, wq_ref[...]).astype(BF16)
    st = _dot_nt(sk_ref[...], q)
    for c in range(nchunk):
        st_ref[c] = st[:, c * LANES:(c + 1) * LANES]
    rk = lax.broadcasted_iota(I32, (nk, LANES), 0)
    r16 = lax.broadcasted_iota(I32, (k, LANES), 0)
    r8 = lax.broadcasted_iota(I32, (8, LANES), 0)

    def top_keys(c, hp):
        s = st_ref[c, pl.ds(pl.multiple_of(hp * nk, nk), nk), :]
        sv = jnp.zeros((k, LANES), F32)
        si = jnp.zeros((k, LANES), I32)
        for it in range(k):
            m = jnp.max(s, axis=0, keepdims=True)
            idx = jnp.min(jnp.where(s == m, rk, nk), axis=0, keepdims=True)
            s = jnp.where(rk == idx, NEG_INF, s)
            sv = jnp.where(r16 == it, m, sv)
            si = jnp.where(r16 == it, idx, si)
        sv_ref[c, pl.ds(pl.multiple_of(hp * k, k), k), :] = sv
        si_ref[c, pl.ds(pl.multiple_of(hp * k, k), k), :] = si

    def stage1(t, _):
        c = t // PEER_HEADS
        hh = t % PEER_HEADS
        top_keys(c, 2 * hh)
        top_keys(c, 2 * hh + 1)
        return 0

    lax.fori_loop(0, nchunk * PEER_HEADS, stage1, 0)

    def top_pairs(c, hh):
        o0 = pl.multiple_of(hh * 2 * k, 2 * k)
        o1 = pl.multiple_of(hh * 2 * k + k, k)
        sv0 = sv_ref[c, pl.ds(o0, k), :]
        sv1 = sv_ref[c, pl.ds(o1, k), :]
        si0 = si_ref[c, pl.ds(o0, k), :]
        si1 = si_ref[c, pl.ds(o1, k), :]
        tiles, codes = [], []
        for a in range(8):
            for b0 in range(0, PEER_CNT[a], 8):
                tile = sv0[a:a + 1, :] + sv1[b0:b0 + 8, :]
                tiles.append(jnp.where(r8 + b0 < PEER_CNT[a], tile, NEG_INF))
                codes.append(a * k + b0 + r8)
        tiles.append(sv0[8:16, :] + sv1[0:1, :])
        codes.append((8 + r8) * k)
        pool = jnp.concatenate(tiles, axis=0)
        code = jnp.concatenate(codes, axis=0)
        ts = jnp.zeros((k, LANES), F32)
        ii = jnp.zeros((k, LANES), I32)
        jj = jnp.zeros((k, LANES), I32)
        for it in range(k):
            m = jnp.max(pool, axis=0, keepdims=True)
            sel = jnp.min(jnp.where(pool == m, code, k * k), axis=0, keepdims=True)
            pool = jnp.where(code == sel, NEG_INF, pool)
            iv = jnp.sum(jnp.where(r16 == (sel >> 4), si0, 0), axis=0, keepdims=True)
            jv = jnp.sum(jnp.where(r16 == (sel & (k - 1)), si1, 0), axis=0, keepdims=True)
            ts = jnp.where(r16 == it, m, ts)
            ii = jnp.where(r16 == it, iv, ii)
            jj = jnp.where(r16 == it, jv, jj)
        e = jnp.exp(ts - jnp.max(ts, axis=0, keepdims=True))
        dst = pl.ds(pl.multiple_of(hh * k, k), k)
        g_ref[c, dst, :] = e / jnp.sum(e, axis=0, keepdims=True)
        i_ref[c, dst, :] = ii
        j_ref[c, dst, :] = jj

    def stage2(t, _):
        c = t // (PEER_HEADS // 2)
        hh = (t % (PEER_HEADS // 2)) * 2
        top_pairs(c, hh)
        top_pairs(c, hh + 1)
        return 0

    lax.fori_loop(0, nchunk * (PEER_HEADS // 2), stage2, 0)


def peer_route(x, shift, scale, wq, sk, *, per_row, rows_per_batch, tm):
    n, d = x.shape
    nchunk = tm // LANES
    npick = PEER_HEADS * PEER_TOPK
    nrow = sk.shape[0]
    pick = lambda dt: jax.ShapeDtypeStruct((n // LANES, npick, LANES), dt)
    pick_spec = pl.BlockSpec((nchunk, npick, LANES), lambda i: (i, 0, 0))
    outs = pl.pallas_call(
        _peer_route_kernel,
        out_shape=(jax.ShapeDtypeStruct((n, d), BF16), pick(F32), pick(I32), pick(I32)),
        grid=(n // tm,),
        in_specs=[pl.BlockSpec((tm, d), lambda i: (i, 0)),
                  _mod_specs(per_row, tm, d, rows_per_batch),
                  _mod_specs(per_row, tm, d, rows_per_batch),
                  pl.BlockSpec(wq.shape, lambda i: (0, 0)),
                  pl.BlockSpec(sk.shape, lambda i: (0, 0))],
        out_specs=(pl.BlockSpec((tm, d), lambda i: (i, 0)),) + (pick_spec,) * 3,
        scratch_shapes=[pltpu.VMEM((nchunk, nrow, LANES), F32),
                        pltpu.VMEM((nchunk, 2 * npick, LANES), F32),
                        pltpu.VMEM((nchunk, 2 * npick, LANES), I32)],
        compiler_params=_cparams(("arbitrary",)),
        name="peer_route",
    )(x, shift, scale, wq, sk)
    to_rows = lambda a: jnp.swapaxes(a, 1, 2).reshape(n, npick)
    return (outs[0],) + tuple(to_rows(a) for a in outs[1:])


def _peer_subkey_matrix(subkeys):
    h, p, nk, dk = subkeys.shape
    eye = jnp.eye(h * p, dtype=F32)
    m = eye[:, None, :, None] * subkeys.reshape(h * p, nk, 1, dk)
    return m.reshape(h * p * nk, h * p * dk).astype(BF16)


PEER_EB = 1024
PEER_UP_CHUNK = 256


def _gather_lanes(x, idx):
    return jnp.take_along_axis(x, idx, axis=1, mode="promise_in_bounds")


def _peer_up_kernel(hb_ref, u_ref, i_ref, j_ref, g_ref, c_ref, acc_ref):
    s = pl.program_id(1)
    tm = hb_ref.shape[0]
    nslab = PEER_EB // LANES

    @pl.when(s == 0)
    def _():
        acc_ref[...] = jnp.zeros(acc_ref.shape, F32)

    hb = hb_ref[...]
    ii = i_ref[...]
    jj = j_ref[...]
    acc = acc_ref[...]
    for ch in range(PEER_EB // PEER_UP_CHUNK):
        a = _dot_nt(hb, u_ref[ch * PEER_UP_CHUNK:(ch + 1) * PEER_UP_CHUNK, :])
        for k in range(PEER_UP_CHUNK // LANES):
            sl = ch * (PEER_UP_CHUNK // LANES) + k
            got = _gather_lanes(a[:, k * LANES:(k + 1) * LANES], jj)
            acc = acc + jnp.where(ii == s * nslab + sl, got, 0.0)
    acc_ref[...] = acc

    @pl.when(s == pl.num_programs(1) - 1)
    def _():
        a = acc_ref[...]
        c_ref[...] = g_ref[...] * (0.5 * a * (1.0 + lax.erf(a * float(2.0 ** -0.5))))


def peer_up(hb, u, ii, jj, g, *, tm):
    n, d = hb.shape
    npick = ii.shape[1]
    pick_spec = pl.BlockSpec((tm, npick), lambda i, s: (i, 0))
    return pl.pallas_call(
        _peer_up_kernel,
        out_shape=jax.ShapeDtypeStruct((n, npick), F32),
        grid=(n // tm, u.shape[0] // PEER_EB),
        in_specs=[pl.BlockSpec((tm, d), lambda i, s: (i, 0)),
                  pl.BlockSpec((PEER_EB, d), lambda i, s: (s, 0)),
                  pick_spec, pick_spec, pick_spec],
        out_specs=pick_spec,
        scratch_shapes=[pltpu.VMEM((tm, npick), F32)],
        compiler_params=_cparams(("arbitrary", "arbitrary")),
        name="peer_up",
    )(hb, u, ii, jj, g)


PEER_TG = 16
PEER_TU = 8


def _peer_down_kernel(x_ref, gate_ref, c_ref, i_ref, j_ref, v_ref, y_ref, wa_ref, stage_ref, acc_ref):
    s = pl.program_id(1)
    tm = x_ref.shape[0]
    nk = PEER_NKEYS
    nslab = PEER_EB // LANES

    @pl.when(s == 0)
    def _():
        acc_ref[...] = jnp.zeros(acc_ref.shape, F32)
        key = lax.broadcasted_iota(I32, (nk, LANES), 0)

        def group(gi, _):
            t0 = pl.multiple_of(gi * PEER_TG, PEER_TG)

            def tokens(u, _):
                for tl in range(PEER_TU):
                    tt = u * PEER_TU + tl
                    row = pl.ds(t0 + tt, 1)
                    ei = jnp.where(key == i_ref[row, :], 1.0, 0.0).astype(BF16)
                    cj = jnp.where(key == j_ref[row, :], c_ref[row, :], 0.0).astype(BF16)
                    stage_ref[tt] = _dot_nt(ei, cj)
                return 0

            lax.fori_loop(0, PEER_TG // PEER_TU, tokens, 0)
            slabs = jnp.swapaxes(stage_ref[...], 0, 1).astype(BF16)
            for i in range(nk):
                wa_ref[i // nslab, pl.ds(t0, PEER_TG), (i % nslab) * LANES:(i % nslab + 1) * LANES] = slabs[i]
            return 0

        lax.fori_loop(0, tm // PEER_TG, group, 0)

    acc_ref[...] += _dot(wa_ref[s], v_ref[...])

    @pl.when(s == pl.num_programs(1) - 1)
    def _():
        y_ref[...] = x_ref[...] + gate_ref[...] * acc_ref[...]


def peer_down(x, gate, c, ii, jj, v, *, per_row, rows_per_batch, tm):
    n, d = x.shape
    npick = c.shape[1]
    nsteps = v.shape[0] // PEER_EB
    pick_spec = pl.BlockSpec((tm, npick), lambda i, s: (i, 0))
    gate_spec = (pl.BlockSpec((tm, d), lambda i, s: (i, 0)) if per_row else
                 pl.BlockSpec((None, 1, d), lambda i, s: ((i * tm) // rows_per_batch, 0, 0)))
    return pl.pallas_call(
        _peer_down_kernel,
        out_shape=jax.ShapeDtypeStruct((n, d), F32),
        grid=(n // tm, nsteps),
        in_specs=[pl.BlockSpec((tm, d), lambda i, s: (i, 0)), gate_spec,
                  pick_spec, pick_spec, pick_spec,
                  pl.BlockSpec((PEER_EB, d), lambda i, s: (s, 0))],
        out_specs=pl.BlockSpec((tm, d), lambda i, s: (i, 0)),
        scratch_shapes=[pltpu.VMEM((nsteps, tm, PEER_EB), BF16),
                        pltpu.VMEM((PEER_TG, PEER_NKEYS, LANES), F32),
                        pltpu.VMEM((tm, d), F32)],
        compiler_params=_cparams(("arbitrary", "arbitrary")),
        name="peer_down",
    )(x, gate, c, ii, jj, v)


def peer_sublayer(x, shift, scale, gate, wq, sk, u, v, *, per_row, rows_per_batch, tm_up, tm_down):
    hb, g, ii, jj = peer_route(x, shift, scale, wq, sk, per_row=per_row, rows_per_batch=rows_per_batch, tm=tm_up)
    c = peer_up(hb, u, ii, jj, g, tm=tm_up)
    return peer_down(x, gate, c, ii, jj, v, per_row=per_row, rows_per_batch=rows_per_batch, tm=tm_down)


TM_PROJ = 256
TM_PEER = 512
TM_PEER_DOWN = 512


def kernel(x_prompt, x_sample, c_prompt, c_sample, cache_mla, cache_dil0, cache_dil1, cache_dil2, page_table, a_mod_w, a_mod_b, a_w_dq, a_g_cq, a_w_uq, a_w_dkv, a_g_ckv, a_g_qn, a_g_qr, a_g_kr, a_w_uk, a_g_kn, a_w_uv, a_w_o, kv_mod_w, kv_mod_b, kv_w, kv_g_k, b_mod_w, b_mod_b, b_w_q, b_g_q, b_w_o, f_mod_w, f_mod_b, f_w_q, f_subkeys, f_u, f_v):
    b, s, d = x_prompt.shape
    db, t, _ = x_sample.shape
    n_p, n_s = b * s, db * t
    past = page_table.shape[1] * PAGE_SIZE
    depth = f_mod_w.shape[0]
    n_a = a_mod_w.shape[0]
    caches = (cache_dil0, cache_dil1, cache_dil2)
    hw = DIL_HEADS * DIL_HEAD_DIM
    lat_w = cache_mla.shape[-1]

    c_all = jnp.concatenate([c_prompt, c_sample], axis=0)
    xp = x_prompt.reshape(n_p, d)
    xs = x_sample.reshape(n_s, d)
    pos_p = jnp.arange(s, dtype=I32)
    pos_s = jnp.tile(past + jnp.arange(t, dtype=I32), db)
    mla_tab_p = _rope_tables(pos_p, MLA_SLAB_ROPE0, MLA_ROPE // 2)
    mla_tab_s = _rope_tables(pos_s, MLA_SLAB_ROPE0, MLA_ROPE // 2)
    dil_tab_p = _rope_tables(pos_p, 0, DIL_ROT // 2)
    dil_tab_s = _rope_tables(pos_s, 0, DIL_ROT // 2)
    kw_p = dict(per_row=False, rows_per_batch=s)
    kw_s = dict(per_row=True, rows_per_batch=t)

    def mods(w, bias):
        m = modulation(c_all, w, bias)
        ms = jnp.repeat(m[b:], t, axis=0)
        k = w.shape[1] // d
        return ([m[:b, None, i * d:(i + 1) * d] for i in range(k)], [ms[:, i * d:(i + 1) * d] for i in range(k)])

    rows_p, rows_s = [], []
    dil_p, dil_s = [], []
    kvf_p = kv_new = None
    for layer in range(depth):
        if layer < n_a:
            i = layer
            (sh_p, sc_p, gt_p), (sh_s, sc_s, gt_s) = mods(a_mod_w[i], a_mod_b[i])
            w = _mla_weights(a_w_dq[i], a_g_cq[i], a_w_uq[i], a_w_dkv[i], a_g_ckv[i], a_g_qn[i], a_g_qr[i],
                             a_g_kr[i], a_w_uk[i], a_g_kn[i], a_w_uv[i])
            wo = a_w_o[i].astype(BF16)
            q, k, v, ckv, kpe = mla_project(xp, sh_p, sc_p, *mla_tab_p, w, tm=TM_PROJ, **kw_p)
            rows_p.append(jnp.concatenate([ckv, kpe[:, MLA_NOPE:MLA_NOPE + MLA_ROPE]], axis=1).reshape(b, s, lat_w))
            o = mla_attention(q, k, v, b, s)
            xp = out_proj(xp, o, wo, gt_p, tm=TM_PROJ, **kw_p)
            q, k, v, ckv, kpe = mla_project(xs, sh_s, sc_s, *mla_tab_s, w, tm=n_s, **kw_s)
            rows_s.append(jnp.concatenate([ckv, kpe[:, MLA_NOPE:MLA_NOPE + MLA_ROPE]], axis=1).reshape(db, t, lat_w))
            o = mla_sample_attention(q, ckv, kpe, cache_mla[i], page_table, a_w_uk[i], a_g_kn[i], a_w_uv[i], t)
            xs = out_proj(xs, o, wo, gt_s, tm=n_s, **kw_s)
        else:
            if layer == n_a:
                (sh_p, sc_p), (sh_s, sc_s) = mods(kv_mod_w, kv_mod_b)
                kvw = kv_w.reshape(d, 2, N_GROUPS, hw)
                wkv = jnp.stack([jnp.concatenate([kvw[:, 0, g], kvw[:, 1, g]], axis=1) for g in range(N_GROUPS)])
                wkv = wkv.astype(BF16)
                gk = kv_g_k[:, None, :]
                kvf_p = ada_heads(xp, sh_p, sc_p, *dil_tab_p, wkv, gk, n_norm=DIL_HEADS, oscale=1.0,
                                  tm=TM_PROJ, **kw_p)
                kvf_s = ada_heads(xs, sh_s, sc_s, *dil_tab_s, wkv, gk, n_norm=DIL_HEADS, oscale=1.0,
                                  tm=n_s, **kw_s)
                new_p = kvf_p.reshape(b, s, N_GROUPS, 2, DIL_HEADS, DIL_HEAD_DIM)
                new_s = kvf_s.reshape(db, t, N_GROUPS, 2, DIL_HEADS, DIL_HEAD_DIM)
                for g in range(N_GROUPS):
                    win = DIL_WINDOWS[g]
                    full_s = jnp.concatenate([caches[g], new_s[:, :, g]], axis=1)
                    dil_p.append(new_p[:, s - min(win, s):, g])
                    dil_s.append(full_s[:, full_s.shape[1] - min(win, past + t):])
                kv_new = jnp.pad(kvf_s.reshape(db, t, -1), ((0, 0), (0, NEW_ROWS - t), (0, 0)))
            j = layer - n_a
            (sh_p, sc_p, gt_p), (sh_s, sc_s, gt_s) = mods(b_mod_w[j], b_mod_b[j])
            wq = jnp.transpose(b_w_q[j].reshape(d, N_GROUPS, hw), (1, 0, 2)).astype(BF16)
            gq = b_g_q[j][:, None, :]
            wo = b_w_o[j].astype(BF16)
            qscale = float(DIL_HEAD_DIM ** -0.5)
            qf = ada_heads(xp, sh_p, sc_p, *dil_tab_p, wq, gq, n_norm=DIL_HEADS, oscale=qscale, tm=TM_PROJ, **kw_p)
            parts = [dil_prompt_attention(qf, kvf_p, g, b, s) for g in range(N_GROUPS)]
            xp = dil_combine(xp, [p[0] for p in parts], [p[1] for p in parts], wo, gt_p, tm=TM_PROJ, **kw_p)
            qf = ada_heads(xs, sh_s, sc_s, *dil_tab_s, wq, gq, n_norm=DIL_HEADS, oscale=qscale, tm=n_s, **kw_s)
            parts = [dil_sample_attention(qf, caches[g], kv_new, g, t) for g in range(N_GROUPS)]
            xs = dil_combine(xs, [p[0] for p in parts], [p[1] for p in parts], wo, gt_s, tm=n_s, **kw_s)
        (sh_p, sc_p, gt_p), (sh_s, sc_s, gt_s) = mods(f_mod_w[layer], f_mod_b[layer])
        wq = f_w_q[layer].astype(BF16)
        sk = _peer_subkey_matrix(f_subkeys[layer])
        u = f_u[layer].astype(BF16)
        v = f_v[layer].astype(BF16)
        xp = peer_sublayer(xp, sh_p, sc_p, gt_p, wq, sk, u, v, tm_up=TM_PEER, tm_down=TM_PEER_DOWN, **kw_p)
        xs = peer_sublayer(xs, sh_s, sc_s, gt_s, wq, sk, u, v, tm_up=n_s, tm_down=n_s, **kw_s)
    return (xp.reshape(b, s, d), xs.reshape(db, t, d), jnp.stack(rows_p), jnp.stack(rows_s),
            dil_p[0], dil_s[0], dil_p[1], dil_s[1], dil_p[2], dil_s[2])
```

```python
import functools

import numpy as np
import jax
import jax.numpy as jnp
from jax import lax
from jax.experimental import pallas as pl
from jax.experimental.pallas import tpu as pltpu

F32 = jnp.float32
BF16 = jnp.bfloat16
I32 = jnp.int32

LANES = 128
VMEM_LIMIT = 56 * 1024 * 1024

ROPE_THETA = 500000.0
EPS = 1e-6
NEG = -1e30

MLA_HEADS = 16
MLA_NOPE = 64
MLA_ROPE = 32
MLA_V = 64
MLA_KV_LORA = 256
PAGE_SIZE = 128

DIL_WINDOWS = (128, 512, 2048)
DIL_RATES = (1, 4, 16)
N_GROUPS = 3
DIL_HEADS = 8
DIL_HEAD_DIM = 128
DIL_ROT = DIL_HEAD_DIM // 4

PEER_HEADS = 8
PEER_NKEYS = 128
PEER_TOPK = 16
PEER_DKEY = 128


def _cparams(sem):
    return pltpu.CompilerParams(dimension_semantics=sem, vmem_limit_bytes=VMEM_LIMIT)


def _rms(x):
    return x * lax.rsqrt(jnp.mean(x * x, axis=-1, keepdims=True) + EPS)


def _dot(a, b):
    return jnp.dot(a, b, preferred_element_type=F32)


def _dot_nt(a, b):
    return lax.dot_general(a, b, (((1,), (1,)), ((), ())), preferred_element_type=F32)


def _mod_specs(per_row, tm, d, rows_per_batch):
    if per_row:
        return pl.BlockSpec((tm, d), lambda i: (i, 0))
    return pl.BlockSpec((None, 1, d), lambda i: ((i * tm) // rows_per_batch, 0, 0))


def _mod_kernel(c_ref, w_ref, b_ref, o_ref):
    c = c_ref[...]
    s = (c * jax.nn.sigmoid(c)).astype(BF16)
    o_ref[...] = _dot(s, w_ref[...].astype(BF16)) + b_ref[...]


def modulation(c, w, b):
    bc, d = c.shape
    n = w.shape[1]
    tn = 1024
    return pl.pallas_call(
        _mod_kernel,
        out_shape=jax.ShapeDtypeStruct((bc, n), F32),
        grid=(n // tn,),
        in_specs=[pl.BlockSpec((bc, d), lambda j: (0, 0)),
                  pl.BlockSpec((d, tn), lambda j: (0, j)),
                  pl.BlockSpec((1, tn), lambda j: (0, j))],
        out_specs=pl.BlockSpec((bc, tn), lambda j: (0, j)),
        compiler_params=_cparams(("arbitrary",)),
        name="modulation",
    )(c, w, b.reshape(1, n))


def _rope_tables(pos, lane0, half):
    inv = ROPE_THETA ** (-jnp.arange(half, dtype=F32) / half)
    ang = pos.astype(F32)[:, None] * inv
    cos, sin = jnp.cos(ang), jnp.sin(ang)
    t = pos.shape[0]
    cos_t = jnp.ones((t, LANES), F32)
    sin_t = jnp.zeros((t, LANES), F32)
    cos_t = cos_t.at[:, lane0:lane0 + half].set(cos).at[:, lane0 + half:lane0 + 2 * half].set(cos)
    sin_t = sin_t.at[:, lane0:lane0 + half].set(-sin).at[:, lane0 + half:lane0 + 2 * half].set(sin)
    return cos_t, sin_t


def _rope_slab(s, cos_t, sin_t, lane, lane0, half):
    swapped = jnp.where(lane < lane0 + half, pltpu.roll(s, LANES - half, 1), pltpu.roll(s, half, 1))
    return s * cos_t + swapped * sin_t


MLA_SLAB_ROPE0 = MLA_NOPE


def _mla_proj_kernel(x_ref, sh_ref, sc_ref, cos_ref, sin_ref, wdq_ref, gcq_ref, wuq_ref, wdkv_ref, gckv_ref,
                     gq_ref, gkr_ref, wuk_ref, gkn_ref, wuv_ref,
                     q_ref, k_ref, v_ref, ckv_ref, kpe_ref, *, qscale):
    x = x_ref[...]
    tm = x.shape[0]
    h = _rms(x) * (1.0 + sc_ref[...]) + sh_ref[...]
    hb = h.astype(BF16)
    cq = _rms(_dot(hb, wdq_ref[...])) * gcq_ref[...]
    q = _dot(cq.astype(BF16), wuq_ref[...])
    kv = _dot(hb, wdkv_ref[...])
    ckv = _rms(kv[:, :MLA_KV_LORA]) * gckv_ref[...]
    cos_t, sin_t = cos_ref[...], sin_ref[...]
    lane = lax.broadcasted_iota(I32, (tm, LANES), 1)
    m_n = lane < MLA_NOPE
    m_r = (lane >= MLA_NOPE) & (lane < MLA_NOPE + MLA_ROPE)

    def norm_rope(s, g):
        s2 = s * s
        ssn = jnp.sum(jnp.where(m_n, s2, 0.0), axis=-1, keepdims=True)
        ssr = jnp.sum(jnp.where(m_r, s2, 0.0), axis=-1, keepdims=True)
        r = jnp.where(m_n, lax.rsqrt(ssn / MLA_NOPE + EPS), lax.rsqrt(ssr / MLA_ROPE + EPS))
        return _rope_slab(s * r * g, cos_t, sin_t, lane, MLA_SLAB_ROPE0, MLA_ROPE // 2)

    kpe = norm_rope(kv[:, MLA_KV_LORA:], gkr_ref[...])
    ckv_ref[...] = ckv
    kpe_ref[...] = kpe
    ckvb = ckv.astype(BF16)
    kn = _dot(ckvb, wuk_ref[...])
    v_ref[...] = _dot(ckvb, wuv_ref[...]).astype(BF16)
    gq = gq_ref[...]
    gkn = gkn_ref[...]
    for hh in range(MLA_HEADS):
        sl = slice(hh * LANES, (hh + 1) * LANES)
        q_ref[hh] = (norm_rope(q[:, sl], gq) * qscale).astype(BF16)
        ks = kn[:, sl]
        ssn = jnp.sum(ks * ks, axis=-1, keepdims=True)
        k_ref[hh] = (ks * lax.rsqrt(ssn / MLA_NOPE + EPS) * gkn + kpe).astype(BF16)


def _mla_weights(w_dq, g_cq, w_uq, w_dkv, g_ckv, g_qn, g_qr, g_kr, w_uk, g_kn, w_uv):
    d = w_dq.shape[0]
    hd = MLA_NOPE + MLA_ROPE
    pad = LANES - hd
    wuq = jnp.pad(w_uq.reshape(-1, MLA_HEADS, hd), ((0, 0), (0, 0), (0, pad))).reshape(-1, MLA_HEADS * LANES)
    wdkv = jnp.concatenate([w_dkv[:, :MLA_KV_LORA], jnp.zeros((d, MLA_NOPE), F32), w_dkv[:, MLA_KV_LORA:],
                            jnp.zeros((d, pad), F32)], axis=1)
    gq = jnp.concatenate([g_qn, g_qr, jnp.zeros((pad,), F32)])[None]
    gkr = jnp.concatenate([jnp.zeros((MLA_NOPE,), F32), g_kr, jnp.zeros((pad,), F32)])[None]
    wuk = jnp.pad(w_uk, ((0, 0), (0, 0), (0, LANES - MLA_NOPE))).reshape(MLA_KV_LORA, MLA_HEADS * LANES)
    gkn = jnp.concatenate([g_kn, jnp.zeros((LANES - MLA_NOPE,), F32)])[None]
    wuv = w_uv.reshape(MLA_KV_LORA, MLA_HEADS * MLA_V)
    return dict(wdq=w_dq.astype(BF16), gcq=g_cq[None], wuq=wuq.astype(BF16), wdkv=wdkv.astype(BF16),
                gckv=g_ckv[None], gq=gq, gkr=gkr, wuk=wuk.astype(BF16), gkn=gkn, wuv=wuv.astype(BF16))


def mla_project(x, shift, scale, cos_t, sin_t, w, *, per_row, rows_per_batch, tm):
    n, d = x.shape
    nblk_pos = cos_t.shape[0] // tm
    full = lambda a: pl.BlockSpec(a.shape, lambda i: (0,) * a.ndim)
    wnames = ("wdq", "gcq", "wuq", "wdkv", "gckv", "gq", "gkr", "wuk", "gkn", "wuv")
    qscale = float((MLA_NOPE + MLA_ROPE) ** -0.5)
    return pl.pallas_call(
        functools.partial(_mla_proj_kernel, qscale=qscale),
        out_shape=(jax.ShapeDtypeStruct((MLA_HEADS, n, LANES), BF16),
                   jax.ShapeDtypeStruct((MLA_HEADS, n, LANES), BF16),
                   jax.ShapeDtypeStruct((n, MLA_HEADS * MLA_V), BF16),
                   jax.ShapeDtypeStruct((n, MLA_KV_LORA), F32),
                   jax.ShapeDtypeStruct((n, LANES), F32)),
        grid=(n // tm,),
        in_specs=[pl.BlockSpec((tm, d), lambda i: (i, 0)),
                  _mod_specs(per_row, tm, d, rows_per_batch),
                  _mod_specs(per_row, tm, d, rows_per_batch),
                  pl.BlockSpec((tm, LANES), lambda i: (i % nblk_pos, 0)),
                  pl.BlockSpec((tm, LANES), lambda i: (i % nblk_pos, 0))] + [full(w[k]) for k in wnames],
        out_specs=(pl.BlockSpec((MLA_HEADS, tm, LANES), lambda i: (0, i, 0)),
                   pl.BlockSpec((MLA_HEADS, tm, LANES), lambda i: (0, i, 0)),
                   pl.BlockSpec((tm, MLA_HEADS * MLA_V), lambda i: (i, 0)),
                   pl.BlockSpec((tm, MLA_KV_LORA), lambda i: (i, 0)),
                   pl.BlockSpec((tm, LANES), lambda i: (i, 0))),
        compiler_params=_cparams(("arbitrary",)),
        name="mla_project",
    )(x, shift, scale, cos_t, sin_t, *[w[k] for k in wnames])


def _mla_attn_kernel(q_ref, k_ref, v_ref, o_ref, *, tq):
    qi = pl.program_id(2)
    row = lax.broadcasted_iota(I32, (tq, tq), 0)
    col = lax.broadcasted_iota(I32, (tq, tq), 1)
    nh = q_ref.shape[0]
    hs = range(nh)
    qs = [q_ref[hh] for hh in hs]

    def step(j, carry, masked):
        off = pl.multiple_of(j * tq, tq)
        vcs = [v_ref[pl.ds(off, tq), hp * LANES:(hp + 1) * LANES] for hp in range(nh // 2)]
        ss = [_dot_nt(qs[hh], k_ref[hh, pl.ds(off, tq), :]) for hh in hs]
        if masked:
            ss = [jnp.where(col <= row, s, NEG) for s in ss]
        m_new = [jnp.maximum(carry[hh][0], jnp.max(ss[hh], axis=-1, keepdims=True)) for hh in hs]
        alpha = [jnp.exp(carry[hh][0] - m_new[hh]) for hh in hs]
        ps = [jnp.exp(ss[hh] - m_new[hh]) for hh in hs]
        ls = [alpha[hh] * carry[hh][1] + jnp.sum(ps[hh], axis=-1, keepdims=True) for hh in hs]
        pv = [_dot(ps[hh].astype(BF16), vcs[hh // 2]) for hh in hs]
        return tuple((m_new[hh], ls[hh], alpha[hh] * carry[hh][2] + pv[hh]) for hh in hs)

    init = (jnp.full((tq, 1), NEG, F32), jnp.zeros((tq, 1), F32), jnp.zeros((tq, LANES), F32))
    carry = lax.fori_loop(0, qi, functools.partial(step, masked=False), (init,) * nh)
    carry = step(qi, carry, masked=True)
    lane = lax.broadcasted_iota(I32, (tq, LANES), 1)
    for hp in range(nh // 2):
        even, odd = carry[2 * hp], carry[2 * hp + 1]
        o_ref[:, hp * LANES:(hp + 1) * LANES] = jnp.where(lane < MLA_V, even[2] / even[1], odd[2] / odd[1]).astype(BF16)


MLA_ATTN_HEADS = 4


def mla_attention(q, k, v, b, s):
    tq = 256
    nh = MLA_ATTN_HEADS
    vw = nh * MLA_V
    q4 = q.reshape(MLA_HEADS, b, s, LANES)
    k4 = k.reshape(MLA_HEADS, b, s, LANES)
    v3 = v.reshape(b, s, MLA_HEADS * MLA_V)
    out = pl.pallas_call(
        functools.partial(_mla_attn_kernel, tq=tq),
        out_shape=jax.ShapeDtypeStruct((b, s, MLA_HEADS * MLA_V), BF16),
        grid=(b, MLA_HEADS // nh, s // tq),
        in_specs=[pl.BlockSpec((nh, None, tq, LANES), lambda bi, h, qi: (h, bi, qi, 0)),
                  pl.BlockSpec((nh, None, s, LANES), lambda bi, h, qi: (h, bi, 0, 0)),
                  pl.BlockSpec((None, s, vw), lambda bi, h, qi: (bi, 0, h))],
        out_specs=pl.BlockSpec((None, tq, vw), lambda bi, h, qi: (bi, qi, h)),
        compiler_params=_cparams(("arbitrary", "arbitrary", "arbitrary")),
        name="mla_attention",
    )(q4, k4, v3)
    return out.reshape(b * s, MLA_HEADS * MLA_V)


def _out_proj_kernel(x_ref, o_ref, w_ref, g_ref, y_ref):
    y_ref[...] = x_ref[...] + g_ref[...] * _dot(o_ref[...], w_ref[...])


def out_proj(x, o, w, gate, *, per_row, rows_per_batch, tm):
    n, d = x.shape
    kdim = o.shape[1]
    return pl.pallas_call(
        _out_proj_kernel,
        out_shape=jax.ShapeDtypeStruct((n, d), F32),
        grid=(n // tm,),
        in_specs=[pl.BlockSpec((tm, d), lambda i: (i, 0)),
                  pl.BlockSpec((tm, kdim), lambda i: (i, 0)),
                  pl.BlockSpec((kdim, d), lambda i: (0, 0)),
                  _mod_specs(per_row, tm, d, rows_per_batch)],
        out_specs=pl.BlockSpec((tm, d), lambda i: (i, 0)),
        compiler_params=_cparams(("arbitrary",)),
        name="out_proj",
    )(x, o, w, gate)


MLA_DEC_PAGES = 8
NEW_ROWS = 8


def _qabs_kernel(q_ref, w_ref, o_ref):
    o_ref[...] = _dot(q_ref[...], w_ref[...]).astype(BF16)


def mla_absorb_q(q, w_uk, g_kn):
    h, n, _ = q.shape
    wt = jnp.transpose(w_uk, (1, 2, 0)) * g_kn[None, :, None]
    wt = jnp.pad(wt, ((0, 0), (0, LANES - MLA_NOPE), (0, 0))).astype(BF16)
    return pl.pallas_call(
        _qabs_kernel,
        out_shape=jax.ShapeDtypeStruct((h, n, MLA_KV_LORA), BF16),
        grid=(h,),
        in_specs=[pl.BlockSpec((None, n, LANES), lambda i: (i, 0, 0)),
                  pl.BlockSpec((None, LANES, MLA_KV_LORA), lambda i: (i, 0, 0))],
        out_specs=pl.BlockSpec((None, n, MLA_KV_LORA), lambda i: (i, 0, 0)),
        compiler_params=_cparams(("arbitrary",)),
        name="mla_absorb_q",
    )(q, wt)


def _mla_decode_kernel(pt_ref, qa_ref, qp_ref, new_ref, wukt_ref, *rest, n_dec):
    page_refs = rest[:MLA_DEC_PAGES]
    o_ref = rest[MLA_DEC_PAGES]
    m_ref, l_ref, acc_ref = rest[MLA_DEC_PAGES + 1:]
    j = pl.program_id(1)
    qa = qa_ref[...]
    qp = qp_ref[...]
    nrow = qa.shape[0]

    def scores(lat_t):
        n = lat_t.shape[1]
        ckvt = lat_t[:MLA_KV_LORA].astype(BF16)
        kpet = lat_t[MLA_KV_LORA:].astype(BF16)
        kn = _dot(wukt_ref[...], ckvt)
        ssq = jnp.sum((kn * kn).reshape(MLA_HEADS, MLA_NOPE, n), axis=1)
        r = lax.rsqrt(ssq / MLA_NOPE + EPS)
        rr = jnp.concatenate([r] * n_dec + [jnp.zeros((nrow - n_dec * MLA_HEADS, n), F32)], axis=0)
        return _dot(qa, ckvt) * rr + _dot(qp, kpet), ckvt

    def accumulate(s, ckvt):
        m = m_ref[...]
        m_new = jnp.maximum(m, jnp.max(s, axis=-1, keepdims=True))
        alpha = jnp.exp(m - m_new)
        p = jnp.exp(s - m_new)
        l_ref[...] = alpha * l_ref[...] + jnp.sum(p, axis=-1, keepdims=True)
        acc_ref[...] = alpha * acc_ref[...] + _dot_nt(p.astype(BF16), ckvt)
        m_ref[...] = m_new

    @pl.when(j == 0)
    def _():
        m_ref[...] = jnp.full(m_ref.shape, NEG, F32)
        l_ref[...] = jnp.zeros(l_ref.shape, F32)
        acc_ref[...] = jnp.zeros(acc_ref.shape, F32)
        s, ckvt = scores(new_ref[...])
        key = lax.broadcasted_iota(I32, s.shape, 1)
        tok = lax.broadcasted_iota(I32, s.shape, 0) // MLA_HEADS
        accumulate(jnp.where((key <= tok) & (key < n_dec), s, NEG), ckvt)

    s, ckvt = scores(jnp.concatenate([r[...] for r in page_refs], axis=1))
    accumulate(s, ckvt)

    @pl.when(j == pl.num_programs(1) - 1)
    def _():
        o_ref[...] = acc_ref[...] / l_ref[...]


def mla_decode_attention(qabs, qpe, lat_new_t, pool_t, page_table, w_uk, n_dec):
    db, n_pages = page_table.shape
    lat = pool_t.shape[1]
    nrow = qabs.shape[1]
    wukt = w_uk.reshape(MLA_KV_LORA, MLA_HEADS * MLA_NOPE).T.astype(BF16)
    steps = n_pages // MLA_DEC_PAGES
    page_specs = [pl.BlockSpec((None, lat, PAGE_SIZE),
                               lambda b, j, pt, pp=pp: (pt[b * n_pages + j * MLA_DEC_PAGES + pp], 0, 0))
                  for pp in range(MLA_DEC_PAGES)]
    grid_spec = pltpu.PrefetchScalarGridSpec(
        num_scalar_prefetch=1,
        grid=(db, steps),
        in_specs=[pl.BlockSpec((None, nrow, MLA_KV_LORA), lambda b, j, pt: (b, 0, 0)),
                  pl.BlockSpec((None, nrow, MLA_ROPE), lambda b, j, pt: (b, 0, 0)),
                  pl.BlockSpec((None, lat, LANES), lambda b, j, pt: (b, 0, 0)),
                  pl.BlockSpec(wukt.shape, lambda b, j, pt: (0, 0))] + page_specs,
        out_specs=pl.BlockSpec((None, nrow, MLA_KV_LORA), lambda b, j, pt: (b, 0, 0)),
        scratch_shapes=[pltpu.VMEM((nrow, 1), F32), pltpu.VMEM((nrow, 1), F32),
                        pltpu.VMEM((nrow, MLA_KV_LORA), F32)],
    )
    return pl.pallas_call(
        functools.partial(_mla_decode_kernel, n_dec=n_dec),
        out_shape=jax.ShapeDtypeStruct((db, nrow, MLA_KV_LORA), F32),
        grid_spec=grid_spec,
        compiler_params=_cparams(("arbitrary", "arbitrary")),
        name="mla_decode_attention",
    )(page_table.reshape(-1), qabs, qpe, lat_new_t, wukt, *([pool_t] * MLA_DEC_PAGES))


def _mla_uv_kernel(x_ref, w_ref, o_ref):
    w = w_ref[...]
    lane = lax.broadcasted_iota(I32, o_ref.shape, 1)
    o_ref[...] = jnp.where(lane < MLA_V, _dot(x_ref[0], w), _dot(x_ref[1], w)).astype(BF16)


def mla_decode_values(x, w_uv):
    h, n, _ = x.shape
    wuv = w_uv.reshape(MLA_KV_LORA, MLA_HEADS * MLA_V).astype(BF16)
    return pl.pallas_call(
        _mla_uv_kernel,
        out_shape=jax.ShapeDtypeStruct((n, MLA_HEADS * MLA_V), BF16),
        grid=(h // 2,),
        in_specs=[pl.BlockSpec((2, n, MLA_KV_LORA), lambda i: (i, 0, 0)),
                  pl.BlockSpec((MLA_KV_LORA, LANES), lambda i: (0, i))],
        out_specs=pl.BlockSpec((n, LANES), lambda i: (0, i)),
        compiler_params=_cparams(("arbitrary",)),
        name="mla_decode_values",
    )(x, wuv)


def mla_sample_attention(q, ckv, kpe, pool, page_table, w_uk, g_kn, w_uv, n_dec):
    db = page_table.shape[0]
    n = db * n_dec
    qabs = mla_absorb_q(q, w_uk, g_kn)
    nq = MLA_HEADS * n_dec
    assert nq <= LANES

    def to_rows(a):
        f = a.shape[-1]
        a = a.reshape(MLA_HEADS, db, n_dec, f).transpose(1, 2, 0, 3).reshape(db, nq, f)
        return jnp.pad(a, ((0, 0), (0, LANES - nq), (0, 0)))

    lat_new = jnp.concatenate([ckv, kpe[:, MLA_NOPE:MLA_NOPE + MLA_ROPE]], axis=1).reshape(db, n_dec, -1)
    lat_new_t = jnp.pad(jnp.swapaxes(lat_new, 1, 2), ((0, 0), (0, 0), (0, LANES - n_dec)))
    out = mla_decode_attention(to_rows(qabs), to_rows(q[:, :, MLA_NOPE:MLA_NOPE + MLA_ROPE]), lat_new_t,
                               jnp.swapaxes(pool, 1, 2), page_table, w_uk, n_dec)
    x = out[:, :nq].reshape(db, n_dec, MLA_HEADS, MLA_KV_LORA).transpose(2, 0, 1, 3)
    x = x.reshape(MLA_HEADS, n, MLA_KV_LORA).astype(BF16)
    return mla_decode_values(x, w_uv)


def _ada_heads_kernel(x_ref, sh_ref, sc_ref, cos_ref, sin_ref, w_ref, g_ref, o_ref, *, n_norm, n_slab, oscale):
    x = x_ref[...]
    tm = x.shape[0]
    h = _rms(x) * (1.0 + sc_ref[...]) + sh_ref[...]
    y = _dot(h.astype(BF16), w_ref[...])
    cos_t, sin_t = cos_ref[...], sin_ref[...]
    g = g_ref[...]
    lane = lax.broadcasted_iota(I32, (tm, LANES), 1)
    for s in range(n_slab):
        sl = slice(s * LANES, (s + 1) * LANES)
        ys = y[:, sl]
        if s < n_norm:
            ys = _rope_slab(_rms(ys) * g, cos_t, sin_t, lane, 0, DIL_ROT // 2) * oscale
        o_ref[:, sl] = ys


def ada_heads(x, shift, scale, cos_t, sin_t, w, g, *, n_norm, oscale, per_row, rows_per_batch, tm):
    n, d = x.shape
    ng, _, gw = w.shape
    n_slab = gw // LANES
    nblk_pos = cos_t.shape[0] // tm
    mod_spec = (pl.BlockSpec((tm, d), lambda gi, i: (i, 0)) if per_row else
                pl.BlockSpec((None, 1, d), lambda gi, i: ((i * tm) // rows_per_batch, 0, 0)))
    return pl.pallas_call(
        functools.partial(_ada_heads_kernel, n_norm=n_norm, n_slab=n_slab, oscale=oscale),
        out_shape=jax.ShapeDtypeStruct((n, ng * gw), F32),
        grid=(ng, n // tm),
        in_specs=[pl.BlockSpec((tm, d), lambda gi, i: (i, 0)), mod_spec, mod_spec,
                  pl.BlockSpec((tm, LANES), lambda gi, i: (i % nblk_pos, 0)),
                  pl.BlockSpec((tm, LANES), lambda gi, i: (i % nblk_pos, 0)),
                  pl.BlockSpec((None, d, gw), lambda gi, i: (gi, 0, 0)),
                  pl.BlockSpec((None, 1, LANES), lambda gi, i: (gi, 0, 0))],
        out_specs=pl.BlockSpec((tm, gw), lambda gi, i: (i, gi)),
        compiler_params=_cparams(("arbitrary", "arbitrary")),
        name="ada_heads",
    )(x, shift, scale, cos_t, sin_t, w, g)


DIL_SPAN = 128
STAT_DEN0 = DIL_HEADS


def _dil_prompt_kernel(q_ref, kp_ref, vp_ref, kc_ref, vc_ref, num_ref, st_ref, *, dil, hps):
    nb = pl.program_id(1)
    hs = pl.program_id(2)
    span = DIL_SPAN
    iq = lax.broadcasted_iota(I32, (span, span), 0)
    jk = lax.broadcasted_iota(I32, (span, span), 1)
    mask_prev = (jk >= iq) & (nb > 0)
    mask_cur = jk <= iq
    lane = lax.broadcasted_iota(I32, (span, LANES), 1)

    @pl.when(hs == 0)
    def _():
        st_ref[...] = jnp.zeros(st_ref.shape, F32)

    def attend(items):
        n = range(len(items))
        qs = [q_ref[rows, sl].astype(BF16) for rows, sl, _ in items]
        sps = [jnp.where(mask_prev, _dot_nt(qs[i], kp_ref[items[i][0], items[i][1]].astype(BF16)), NEG) for i in n]
        scs = [jnp.where(mask_cur, _dot_nt(qs[i], kc_ref[items[i][0], items[i][1]].astype(BF16)), NEG) for i in n]
        ms = [jnp.maximum(jnp.max(sps[i], axis=-1, keepdims=True), jnp.max(scs[i], axis=-1, keepdims=True)) for i in n]
        pps = [jnp.exp(sps[i] - ms[i]) for i in n]
        pcs = [jnp.exp(scs[i] - ms[i]) for i in n]
        dens = [jnp.sum(pps[i], axis=-1, keepdims=True) + jnp.sum(pcs[i], axis=-1, keepdims=True) for i in n]
        for i in n:
            rows, sl, hh = items[i]
            num_ref[rows, sl] = (_dot(pps[i].astype(BF16), vp_ref[rows, sl].astype(BF16))
                                 + _dot(pcs[i].astype(BF16), vc_ref[rows, sl].astype(BF16)))
        for i in n:
            rows, _, hh = items[i]
            stats = st_ref[rows, :]
            stats = jnp.where(lane == hh, ms[i], stats)
            st_ref[rows, :] = jnp.where(lane == STAT_DEN0 + hh, dens[i], stats)

    if dil > 1:
        per_trip = 2 if dil < 8 else 4

        def residues(u, _):
            attend([(pl.ds(u * per_trip + k, span, stride=dil), slice(0, LANES), hs) for k in range(per_trip)])
            return 0

        lax.fori_loop(0, dil // per_trip, residues, 0)
    else:
        for h0 in range(0, hps, 4):
            attend([(pl.ds(0, span), slice(hl * LANES, (hl + 1) * LANES), hs * hps + hl) for hl in range(h0, h0 + 4)])


DIL_HEADS_PER_STEP = (8, 1, 1)


def dil_prompt_attention(q, kv, g, b, s):
    dil = DIL_RATES[g]
    span = DIL_WINDOWS[g] // dil
    assert span == DIL_SPAN and s % (dil * span) == 0
    rows = span * dil
    n_blk = s // rows
    hps = DIL_HEADS_PER_STEP[g]
    hw = DIL_HEADS * DIL_HEAD_DIM
    bw = hps * DIL_HEAD_DIM
    nh = DIL_HEADS // hps
    q3 = q.reshape(b, s, q.shape[1])
    kv3 = kv.reshape(b, s, kv.shape[1])
    prev = lambda nb: jnp.maximum(nb - 1, 0)
    blk = lambda row_of, col0: pl.BlockSpec((None, rows, bw), lambda bi, nb, hs: (bi, row_of(nb), col0 * nh + hs))
    cur = lambda nb: nb
    num, st = pl.pallas_call(
        functools.partial(_dil_prompt_kernel, dil=dil, hps=hps),
        out_shape=(jax.ShapeDtypeStruct((b, s, hw), F32), jax.ShapeDtypeStruct((b, s, LANES), F32)),
        grid=(b, n_blk, nh),
        in_specs=[blk(cur, g), blk(prev, 2 * g), blk(prev, 2 * g + 1), blk(cur, 2 * g), blk(cur, 2 * g + 1)],
        out_specs=(blk(cur, 0), pl.BlockSpec((None, rows, LANES), lambda bi, nb, hs: (bi, nb, 0))),
        compiler_params=_cparams(("arbitrary", "arbitrary", "arbitrary")),
        name=f"dil_prompt_attention_g{g}",
    )(q3, kv3, kv3, kv3, kv3)
    return num.reshape(b * s, hw), st.reshape(b * s, LANES)


def _dil_sample_kernel(q_ref, cache_ref, new_ref, num_ref, st_ref, *, dil, n_dec):
    t = pl.program_id(1)
    hw = DIL_HEADS * DIL_HEAD_DIM
    rows = cache_ref.shape[0]
    n_new = new_ref.shape[0]
    crow = lax.broadcasted_iota(I32, (rows, 1), 0)
    nrow = lax.broadcasted_iota(I32, (n_new, 1), 0)
    cache_ok = (crow >= t) if dil == 1 else (crow >= 0)
    new_ok = (nrow <= t) & (nrow < n_dec) if dil == 1 else (nrow == t)
    lane = lax.broadcasted_iota(I32, (1, LANES), 1)
    stats = jnp.zeros((1, LANES), F32)
    for hh in range(DIL_HEADS):
        sl = slice(hh * LANES, (hh + 1) * LANES)
        q = q_ref[:, sl]
        sc = jnp.where(cache_ok, jnp.sum(cache_ref[:, 0, hh, :] * q, axis=-1, keepdims=True), NEG)
        sn = jnp.where(new_ok, jnp.sum(new_ref[:, sl] * q, axis=-1, keepdims=True), NEG)
        m = jnp.maximum(jnp.max(sc, axis=0, keepdims=True), jnp.max(sn, axis=0, keepdims=True))
        pc = jnp.exp(sc - m)
        pn = jnp.exp(sn - m)
        den = jnp.sum(pc, axis=0, keepdims=True) + jnp.sum(pn, axis=0, keepdims=True)
        vs = slice(hw + hh * LANES, hw + (hh + 1) * LANES)
        num_ref[:, sl] = (jnp.sum(pc * cache_ref[:, 1, hh, :], axis=0, keepdims=True)
                          + jnp.sum(pn * new_ref[:, vs], axis=0, keepdims=True))
        stats = jnp.where(lane == hh, m, stats)
        stats = jnp.where(lane == STAT_DEN0 + hh, den, stats)
    st_ref[...] = stats


def dil_sample_attention(q, cache, kv_new, g, n_dec):
    dil = DIL_RATES[g]
    db, w = cache.shape[:2]
    assert w == DIL_WINDOWS[g] and w // dil == DIL_SPAN and (dil == 1 or n_dec <= dil)
    hw = DIL_HEADS * DIL_HEAD_DIM
    rows = w // dil
    cv = cache.reshape(db, rows, dil, 2, DIL_HEADS, DIL_HEAD_DIM)
    q4 = q.reshape(db, n_dec, 1, q.shape[1])
    n_new = kv_new.shape[1]
    num, st = pl.pallas_call(
        functools.partial(_dil_sample_kernel, dil=dil, n_dec=n_dec),
        out_shape=(jax.ShapeDtypeStruct((db, n_dec, 1, hw), F32),
                   jax.ShapeDtypeStruct((db, n_dec, 1, LANES), F32)),
        grid=(db, n_dec),
        in_specs=[pl.BlockSpec((None, None, 1, hw), lambda bi, t: (bi, t, 0, g)),
                  pl.BlockSpec((None, rows, None, 2, DIL_HEADS, DIL_HEAD_DIM),
                               lambda bi, t: (bi, 0, t if dil > 1 else 0, 0, 0, 0)),
                  pl.BlockSpec((None, n_new, 2 * hw), lambda bi, t: (bi, 0, g))],
        out_specs=(pl.BlockSpec((None, None, 1, hw), lambda bi, t: (bi, t, 0, 0)),
                   pl.BlockSpec((None, None, 1, LANES), lambda bi, t: (bi, t, 0, 0))),
        compiler_params=_cparams(("arbitrary", "arbitrary")),
        name=f"dil_sample_attention_g{g}",
    )(q4, cv, kv_new)
    return num.reshape(db * n_dec, hw), st.reshape(db * n_dec, LANES)


def _dil_combine_kernel(x_ref, n0_ref, n1_ref, n2_ref, s0_ref, s1_ref, s2_ref, w_ref, g_ref, y_ref):
    nums = (n0_ref, n1_ref, n2_ref)
    stats = (s0_ref[...], s1_ref[...], s2_ref[...])
    outs = []
    for hh in range(DIL_HEADS):
        sl = slice(hh * LANES, (hh + 1) * LANES)
        ms = [st[:, hh:hh + 1] for st in stats]
        ds = [st[:, STAT_DEN0 + hh:STAT_DEN0 + hh + 1] for st in stats]
        big = jnp.maximum(jnp.maximum(ms[0], ms[1]), ms[2])
        num = 0.0
        den = 0.0
        for gi in range(N_GROUPS):
            wgt = jnp.exp(ms[gi] - big)
            num = num + wgt * nums[gi][:, sl]
            den = den + wgt * ds[gi]
        outs.append((num / den).astype(BF16))
    o = jnp.concatenate(outs, axis=1)
    y_ref[...] = x_ref[...] + g_ref[...] * _dot(o, w_ref[...])


def dil_combine(x, nums, stats, w, gate, *, per_row, rows_per_batch, tm):
    n, d = x.shape
    hw = DIL_HEADS * DIL_HEAD_DIM
    row = lambda width: pl.BlockSpec((tm, width), lambda i: (i, 0))
    return pl.pallas_call(
        _dil_combine_kernel,
        out_shape=jax.ShapeDtypeStruct((n, d), F32),
        grid=(n // tm,),
        in_specs=[row(d)] + [row(hw)] * 3 + [row(LANES)] * 3 + [pl.BlockSpec((hw, d), lambda i: (0, 0)),
                                                                 _mod_specs(per_row, tm, d, rows_per_batch)],
        out_specs=row(d),
        compiler_params=_cparams(("arbitrary",)),
        name="dil_combine",
    )(x, *nums, *stats, w, gate)


PEER_CNT = tuple(PEER_TOPK // (a + 1) for a in range(PEER_TOPK))
NEG_INF = float("-inf")


def _peer_route_kernel(x_ref, sh_ref, sc_ref, wq_ref, sk_ref,
                       hb_ref, g_ref, i_ref, j_ref,
                       st_ref, sv_ref, si_ref):
    k = PEER_TOPK
    nk = PEER_NKEYS
    x = x_ref[...]
    tm = x.shape[0]
    nchunk = tm // LANES
    h = _rms(x) * (1.0 + sc_ref[...]) + sh_ref[...]
    hb = h.astype(BF16)
    hb_ref[...] = hb
    q = _dot(hb, wq_ref[...]).astype(BF16)
    st = _dot_nt(sk_ref[...], q)
    for c in range(nchunk):
        st_ref[c] = st[:, c * LANES:(c + 1) * LANES]
    rk = lax.broadcasted_iota(I32, (nk, LANES), 0)
    r16 = lax.broadcasted_iota(I32, (k, LANES), 0)
    r8 = lax.broadcasted_iota(I32, (8, LANES), 0)

    def top_keys(c, hp):
        s = st_ref[c, pl.ds(pl.multiple_of(hp * nk, nk), nk), :]
        sv = jnp.zeros((k, LANES), F32)
        si = jnp.zeros((k, LANES), I32)
        for it in range(k):
            m = jnp.max(s, axis=0, keepdims=True)
            idx = jnp.min(jnp.where(s == m, rk, nk), axis=0, keepdims=True)
            s = jnp.where(rk == idx, NEG_INF, s)
            sv = jnp.where(r16 == it, m, sv)
            si = jnp.where(r16 == it, idx, si)
        sv_ref[c, pl.ds(pl.multiple_of(hp * k, k), k), :] = sv
        si_ref[c, pl.ds(pl.multiple_of(hp * k, k), k), :] = si

    def stage1(t, _):
        c = t // PEER_HEADS
        hh = t % PEER_HEADS
        top_keys(c, 2 * hh)
        top_keys(c, 2 * hh + 1)
        return 0

    lax.fori_loop(0, nchunk * PEER_HEADS, stage1, 0)

    def top_pairs(c, hh):
        o0 = pl.multiple_of(hh * 2 * k, 2 * k)
        o1 = pl.multiple_of(hh * 2 * k + k, k)
        sv0 = sv_ref[c, pl.ds(o0, k), :]
        sv1 = sv_ref[c, pl.ds(o1, k), :]
        si0 = si_ref[c, pl.ds(o0, k), :]
        si1 = si_ref[c, pl.ds(o1, k), :]
        tiles, codes = [], []
        for a in range(8):
            for b0 in range(0, PEER_CNT[a], 8):
                tile = sv0[a:a + 1, :] + sv1[b0:b0 + 8, :]
                tiles.append(jnp.where(r8 + b0 < PEER_CNT[a], tile, NEG_INF))
                codes.append(a * k + b0 + r8)
        tiles.append(sv0[8:16, :] + sv1[0:1, :])
        codes.append((8 + r8) * k)
        pool = jnp.concatenate(tiles, axis=0)
        code = jnp.concatenate(codes, axis=0)
        ts = jnp.zeros((k, LANES), F32)
        ii = jnp.zeros((k, LANES), I32)
        jj = jnp.zeros((k, LANES), I32)
        for it in range(k):
            m = jnp.max(pool, axis=0, keepdims=True)
            sel = jnp.min(jnp.where(pool == m, code, k * k), axis=0, keepdims=True)
            pool = jnp.where(code == sel, NEG_INF, pool)
            iv = jnp.sum(jnp.where(r16 == (sel >> 4), si0, 0), axis=0, keepdims=True)
            jv = jnp.sum(jnp.where(r16 == (sel & (k - 1)), si1, 0), axis=0, keepdims=True)
            ts = jnp.where(r16 == it, m, ts)
            ii = jnp.where(r16 == it, iv, ii)
            jj = jnp.where(r16 == it, jv, jj)
        e = jnp.exp(ts - jnp.max(ts, axis=0, keepdims=True))
        dst = pl.ds(pl.multiple_of(hh * k, k), k)
        g_ref[c, dst, :] = e / jnp.sum(e, axis=0, keepdims=True)
        i_ref[c, dst, :] = ii
        j_ref[c, dst, :] = jj

    def stage2(t, _):
        c = t // (PEER_HEADS // 2)
        hh = (t % (PEER_HEADS // 2)) * 2
        top_pairs(c, hh)
        top_pairs(c, hh + 1)
        return 0

    lax.fori_loop(0, nchunk * (PEER_HEADS // 2), stage2, 0)


def peer_route(x, shift, scale, wq, sk, *, per_row, rows_per_batch, tm):
    n, d = x.shape
    nchunk = tm // LANES
    npick = PEER_HEADS * PEER_TOPK
    nrow = sk.shape[0]
    pick = lambda dt: jax.ShapeDtypeStruct((n // LANES, npick, LANES), dt)
    pick_spec = pl.BlockSpec((nchunk, npick, LANES), lambda i: (i, 0, 0))
    outs = pl.pallas_call(
        _peer_route_kernel,
        out_shape=(jax.ShapeDtypeStruct((n, d), BF16), pick(F32), pick(I32), pick(I32)),
        grid=(n // tm,),
        in_specs=[pl.BlockSpec((tm, d), lambda i: (i, 0)),
                  _mod_specs(per_row, tm, d, rows_per_batch),
                  _mod_specs(per_row, tm, d, rows_per_batch),
                  pl.BlockSpec(wq.shape, lambda i: (0, 0)),
                  pl.BlockSpec(sk.shape, lambda i: (0, 0))],
        out_specs=(pl.BlockSpec((tm, d), lambda i: (i, 0)),) + (pick_spec,) * 3,
        scratch_shapes=[pltpu.VMEM((nchunk, nrow, LANES), F32),
                        pltpu.VMEM((nchunk, 2 * npick, LANES), F32),
                        pltpu.VMEM((nchunk, 2 * npick, LANES), I32)],
        compiler_params=_cparams(("arbitrary",)),
        name="peer_route",
    )(x, shift, scale, wq, sk)
    to_rows = lambda a: jnp.swapaxes(a, 1, 2).reshape(n, npick)
    return (outs[0],) + tuple(to_rows(a) for a in outs[1:])


def _peer_subkey_matrix(subkeys):
    h, p, nk, dk = subkeys.shape
    eye = jnp.eye(h * p, dtype=F32)
    m = eye[:, None, :, None] * subkeys.reshape(h * p, nk, 1, dk)
    return m.reshape(h * p * nk, h * p * dk).astype(BF16)


PEER_EB = 1024
PEER_UP_CHUNK = 256


def _gather_lanes(x, idx):
    return jnp.take_along_axis(x, idx, axis=1, mode="promise_in_bounds")


def _peer_up_kernel(hb_ref, u_ref, i_ref, j_ref, g_ref, c_ref, acc_ref):
    s = pl.program_id(1)
    tm = hb_ref.shape[0]
    nslab = PEER_EB // LANES

    @pl.when(s == 0)
    def _():
        acc_ref[...] = jnp.zeros(acc_ref.shape, F32)

    hb = hb_ref[...]
    ii = i_ref[...]
    jj = j_ref[...]
    acc = acc_ref[...]
    for ch in range(PEER_EB // PEER_UP_CHUNK):
        a = _dot_nt(hb, u_ref[ch * PEER_UP_CHUNK:(ch + 1) * PEER_UP_CHUNK, :])
        for k in range(PEER_UP_CHUNK // LANES):
            sl = ch * (PEER_UP_CHUNK // LANES) + k
            got = _gather_lanes(a[:, k * LANES:(k + 1) * LANES], jj)
            acc = acc + jnp.where(ii == s * nslab + sl, got, 0.0)
    acc_ref[...] = acc

    @pl.when(s == pl.num_programs(1) - 1)
    def _():
        a = acc_ref[...]
        c_ref[...] = g_ref[...] * (0.5 * a * (1.0 + lax.erf(a * float(2.0 ** -0.5))))


def peer_up(hb, u, ii, jj, g, *, tm):
    n, d = hb.shape
    npick = ii.shape[1]
    pick_spec = pl.BlockSpec((tm, npick), lambda i, s: (i, 0))
    return pl.pallas_call(
        _peer_up_kernel,
        out_shape=jax.ShapeDtypeStruct((n, npick), F32),
        grid=(n // tm, u.shape[0] // PEER_EB),
        in_specs=[pl.BlockSpec((tm, d), lambda i, s: (i, 0)),
                  pl.BlockSpec((PEER_EB, d), lambda i, s: (s, 0)),
                  pick_spec, pick_spec, pick_spec],
        out_specs=pick_spec,
        scratch_shapes=[pltpu.VMEM((tm, npick), F32)],
        compiler_params=_cparams(("arbitrary", "arbitrary")),
        name="peer_up",
    )(hb, u, ii, jj, g)


PEER_TG = 16
PEER_TU = 8


def _peer_down_kernel(x_ref, gate_ref, c_ref, i_ref, j_ref, v_ref, y_ref, wa_ref, stage_ref, acc_ref, *, nsteps):
    r = pl.program_id(0)
    s = pl.program_id(1)
    tm = x_ref.shape[0]
    nk = PEER_NKEYS
    nslab = PEER_EB // LANES
    ngroups = tm // PEER_TG
    build = r % 2
    straight_line = ngroups >= nsteps

    @pl.when((r == 0) & (s == 0))
    def _():
        wa_ref[1] = jnp.zeros(wa_ref.shape[1:], BF16)
        acc_ref[...] = jnp.zeros(acc_ref.shape, F32)

    key = lax.broadcasted_iota(I32, (nk, LANES), 0)

    def build_group(gi):
        t0 = pl.multiple_of(gi * PEER_TG, PEER_TG)

        def tokens(u, _):
            for tl in range(PEER_TU):
                tt = u * PEER_TU + tl
                row = pl.ds(t0 + tt, 1)
                ei = jnp.where(key == i_ref[row, :], 1.0, 0.0).astype(BF16)
                cj = jnp.where(key == j_ref[row, :], c_ref[row, :], 0.0).astype(BF16)
                stage_ref[tt] = _dot_nt(ei, cj)
            return 0

        if straight_line:
            for u in range(PEER_TG // PEER_TU):
                tokens(u, 0)
        else:
            lax.fori_loop(0, PEER_TG // PEER_TU, tokens, 0)
        slabs = jnp.swapaxes(stage_ref[...], 0, 1).astype(BF16)
        for i in range(nk):
            wa_ref[build, i // nslab, pl.ds(t0, PEER_TG), (i % nslab) * LANES:(i % nslab + 1) * LANES] = slabs[i]

    if ngroups >= nsteps:
        for k in range(ngroups // nsteps):
            build_group(s * (ngroups // nsteps) + k)
    else:
        pl.when(s < ngroups)(lambda: build_group(s))

    acc = jnp.where(s == 0, 0.0, acc_ref[...]) + _dot(wa_ref[1 - build, s], v_ref[...])
    acc_ref[...] = acc

    @pl.when(s == nsteps - 1)
    def _():
        y_ref[...] = x_ref[...] + gate_ref[...] * acc_ref[...]


def peer_down(x, gate, c, ii, jj, v, *, per_row, rows_per_batch, tm):
    n, d = x.shape
    npick = c.shape[1]
    nsteps = v.shape[0] // PEER_EB
    ntiles = n // tm
    ngroups = tm // PEER_TG
    assert ngroups % nsteps == 0 or ngroups < nsteps
    built = lambda r: jnp.minimum(r, ntiles - 1)
    used = lambda r: jnp.maximum(r - 1, 0)
    pick_spec = pl.BlockSpec((tm, npick), lambda r, s: (built(r), 0))
    gate_spec = (pl.BlockSpec((tm, d), lambda r, s: (used(r), 0)) if per_row else
                 pl.BlockSpec((None, 1, d), lambda r, s: ((used(r) * tm) // rows_per_batch, 0, 0)))
    return pl.pallas_call(
        functools.partial(_peer_down_kernel, nsteps=nsteps),
        out_shape=jax.ShapeDtypeStruct((n, d), F32),
        grid=(ntiles + 1, nsteps),
        in_specs=[pl.BlockSpec((tm, d), lambda r, s: (used(r), 0)), gate_spec,
                  pick_spec, pick_spec, pick_spec,
                  pl.BlockSpec((PEER_EB, d), lambda r, s: (s, 0))],
        out_specs=pl.BlockSpec((tm, d), lambda r, s: (used(r), 0)),
        scratch_shapes=[pltpu.VMEM((2, nsteps, tm, PEER_EB), BF16),
                        pltpu.VMEM((PEER_TG, PEER_NKEYS, LANES), F32),
                        pltpu.VMEM((tm, d), F32)],
        compiler_params=_cparams(("arbitrary", "arbitrary")),
        name="peer_down",
    )(x, gate, c, ii, jj, v)


def peer_sublayer(x, shift, scale, gate, wq, sk, u, v, *, per_row, rows_per_batch, tm_up, tm_down):
    hb, g, ii, jj = peer_route(x, shift, scale, wq, sk, per_row=per_row, rows_per_batch=rows_per_batch, tm=tm_up)
    c = peer_up(hb, u, ii, jj, g, tm=tm_up)
    return peer_down(x, gate, c, ii, jj, v, per_row=per_row, rows_per_batch=rows_per_batch, tm=tm_down)


TM_PROJ = 256
TM_PEER = 512
TM_PEER_DOWN = 256


def kernel(x_prompt, x_sample, c_prompt, c_sample, cache_mla, cache_dil0, cache_dil1, cache_dil2, page_table, a_mod_w, a_mod_b, a_w_dq, a_g_cq, a_w_uq, a_w_dkv, a_g_ckv, a_g_qn, a_g_qr, a_g_kr, a_w_uk, a_g_kn, a_w_uv, a_w_o, kv_mod_w, kv_mod_b, kv_w, kv_g_k, b_mod_w, b_mod_b, b_w_q, b_g_q, b_w_o, f_mod_w, f_mod_b, f_w_q, f_subkeys, f_u, f_v):
    b, s, d = x_prompt.shape
    db, t, _ = x_sample.shape
    n_p, n_s = b * s, db * t
    past = page_table.shape[1] * PAGE_SIZE
    depth = f_mod_w.shape[0]
    n_a = a_mod_w.shape[0]
    caches = (cache_dil0, cache_dil1, cache_dil2)
    hw = DIL_HEADS * DIL_HEAD_DIM
    lat_w = cache_mla.shape[-1]

    c_all = jnp.concatenate([c_prompt, c_sample], axis=0)
    xp = x_prompt.reshape(n_p, d)
    xs = x_sample.reshape(n_s, d)
    pos_p = jnp.arange(s, dtype=I32)
    pos_s = jnp.tile(past + jnp.arange(t, dtype=I32), db)
    mla_tab_p = _rope_tables(pos_p, MLA_SLAB_ROPE0, MLA_ROPE // 2)
    mla_tab_s = _rope_tables(pos_s, MLA_SLAB_ROPE0, MLA_ROPE // 2)
    dil_tab_p = _rope_tables(pos_p, 0, DIL_ROT // 2)
    dil_tab_s = _rope_tables(pos_s, 0, DIL_ROT // 2)
    kw_p = dict(per_row=False, rows_per_batch=s)
    kw_s = dict(per_row=True, rows_per_batch=t)

    def mods(w, bias):
        m = modulation(c_all, w, bias)
        ms = jnp.repeat(m[b:], t, axis=0)
        k = w.shape[1] // d
        return ([m[:b, None, i * d:(i + 1) * d] for i in range(k)], [ms[:, i * d:(i + 1) * d] for i in range(k)])

    rows_p, rows_s = [], []
    dil_p, dil_s = [], []
    kvf_p = kv_new = None
    for layer in range(depth):
        if layer < n_a:
            i = layer
            (sh_p, sc_p, gt_p), (sh_s, sc_s, gt_s) = mods(a_mod_w[i], a_mod_b[i])
            w = _mla_weights(a_w_dq[i], a_g_cq[i], a_w_uq[i], a_w_dkv[i], a_g_ckv[i], a_g_qn[i], a_g_qr[i],
                             a_g_kr[i], a_w_uk[i], a_g_kn[i], a_w_uv[i])
            wo = a_w_o[i].astype(BF16)
            q, k, v, ckv, kpe = mla_project(xp, sh_p, sc_p, *mla_tab_p, w, tm=TM_PROJ, **kw_p)
            rows_p.append(jnp.concatenate([ckv, kpe[:, MLA_NOPE:MLA_NOPE + MLA_ROPE]], axis=1).reshape(b, s, lat_w))
            o = mla_attention(q, k, v, b, s)
            xp = out_proj(xp, o, wo, gt_p, tm=TM_PROJ, **kw_p)
            q, k, v, ckv, kpe = mla_project(xs, sh_s, sc_s, *mla_tab_s, w, tm=n_s, **kw_s)
            rows_s.append(jnp.concatenate([ckv, kpe[:, MLA_NOPE:MLA_NOPE + MLA_ROPE]], axis=1).reshape(db, t, lat_w))
            o = mla_sample_attention(q, ckv, kpe, cache_mla[i], page_table, a_w_uk[i], a_g_kn[i], a_w_uv[i], t)
            xs = out_proj(xs, o, wo, gt_s, tm=n_s, **kw_s)
        else:
            if layer == n_a:
                (sh_p, sc_p), (sh_s, sc_s) = mods(kv_mod_w, kv_mod_b)
                kvw = kv_w.reshape(d, 2, N_GROUPS, hw)
                wkv = jnp.stack([jnp.concatenate([kvw[:, 0, g], kvw[:, 1, g]], axis=1) for g in range(N_GROUPS)])
                wkv = wkv.astype(BF16)
                gk = kv_g_k[:, None, :]
                kvf_p = ada_heads(xp, sh_p, sc_p, *dil_tab_p, wkv, gk, n_norm=DIL_HEADS, oscale=1.0,
                                  tm=TM_PROJ, **kw_p)
                kvf_s = ada_heads(xs, sh_s, sc_s, *dil_tab_s, wkv, gk, n_norm=DIL_HEADS, oscale=1.0,
                                  tm=n_s, **kw_s)
                new_p = kvf_p.reshape(b, s, N_GROUPS, 2, DIL_HEADS, DIL_HEAD_DIM)
                new_s = kvf_s.reshape(db, t, N_GROUPS, 2, DIL_HEADS, DIL_HEAD_DIM)
                for g in range(N_GROUPS):
                    win = DIL_WINDOWS[g]
                    full_s = jnp.concatenate([caches[g], new_s[:, :, g]], axis=1)
                    dil_p.append(new_p[:, s - min(win, s):, g])
                    dil_s.append(full_s[:, full_s.shape[1] - min(win, past + t):])
                kv_new = jnp.pad(kvf_s.reshape(db, t, -1), ((0, 0), (0, NEW_ROWS - t), (0, 0)))
            j = layer - n_a
            (sh_p, sc_p, gt_p), (sh_s, sc_s, gt_s) = mods(b_mod_w[j], b_mod_b[j])
            wq = jnp.transpose(b_w_q[j].reshape(d, N_GROUPS, hw), (1, 0, 2)).astype(BF16)
            gq = b_g_q[j][:, None, :]
            wo = b_w_o[j].astype(BF16)
            qscale = float(DIL_HEAD_DIM ** -0.5)
            qf = ada_heads(xp, sh_p, sc_p, *dil_tab_p, wq, gq, n_norm=DIL_HEADS, oscale=qscale, tm=TM_PROJ, **kw_p)
            parts = [dil_prompt_attention(qf, kvf_p, g, b, s) for g in range(N_GROUPS)]
            xp = dil_combine(xp, [p[0] for p in parts], [p[1] for p in parts], wo, gt_p, tm=TM_PROJ, **kw_p)
            qf = ada_heads(xs, sh_s, sc_s, *dil_tab_s, wq, gq, n_norm=DIL_HEADS, oscale=qscale, tm=n_s, **kw_s)
            parts = [dil_sample_attention(qf, caches[g], kv_new, g, t) for g in range(N_GROUPS)]
            xs = dil_combine(xs, [p[0] for p in parts], [p[1] for p in parts], wo, gt_s, tm=n_s, **kw_s)
        (sh_p, sc_p, gt_p), (sh_s, sc_s, gt_s) = mods(f_mod_w[layer], f_mod_b[layer])
        wq = f_w_q[layer].astype(BF16)
        sk = _peer_subkey_matrix(f_subkeys[layer])
        u = f_u[layer].astype(BF16)
        v = f_v[layer].astype(BF16)
        xp = peer_sublayer(xp, sh_p, sc_p, gt_p, wq, sk, u, v, tm_up=TM_PEER, tm_down=TM_PEER_DOWN, **kw_p)
        xs = peer_sublayer(xs, sh_s, sc_s, gt_s, wq, sk, u, v, tm_up=n_s, tm_down=n_s, **kw_s)
    return (xp.reshape(b, s, d), xs.reshape(db, t, d), jnp.stack(rows_p), jnp.stack(rows_s),
            dil_p[0], dil_s[0], dil_p[1], dil_s[1], dil_p[2], dil_s[2])
```

```python
import functools

import numpy as np
import jax
import jax.numpy as jnp
from jax import lax
from jax.experimental import pallas as pl
from jax.experimental.pallas import tpu as pltpu

F32 = jnp.float32
BF16 = jnp.bfloat16
I32 = jnp.int32

LANES = 128
VMEM_LIMIT = 56 * 1024 * 1024

ROPE_THETA = 500000.0
EPS = 1e-6
NEG = -1e30

MLA_HEADS = 16
MLA_NOPE = 64
MLA_ROPE = 32
MLA_V = 64
MLA_KV_LORA = 256
PAGE_SIZE = 128

DIL_WINDOWS = (128, 512, 2048)
DIL_RATES = (1, 4, 16)
N_GROUPS = 3
DIL_HEADS = 8
DIL_HEAD_DIM = 128
DIL_ROT = DIL_HEAD_DIM // 4

PEER_HEADS = 8
PEER_NKEYS = 128
PEER_TOPK = 16
PEER_DKEY = 128


def _cparams(sem):
    return pltpu.CompilerParams(dimension_semantics=sem, vmem_limit_bytes=VMEM_LIMIT)


def _rms(x):
    return x * lax.rsqrt(jnp.mean(x * x, axis=-1, keepdims=True) + EPS)


def _dot(a, b):
    return jnp.dot(a, b, preferred_element_type=F32)


def _dot_nt(a, b):
    return lax.dot_general(a, b, (((1,), (1,)), ((), ())), preferred_element_type=F32)


def _mod_specs(per_row, tm, d, rows_per_batch):
    if per_row:
        return pl.BlockSpec((tm, d), lambda i: (i, 0))
    return pl.BlockSpec((None, 1, d), lambda i: ((i * tm) // rows_per_batch, 0, 0))


def _mod_kernel(c_ref, w_ref, b_ref, o_ref):
    c = c_ref[...]
    s = (c * jax.nn.sigmoid(c)).astype(BF16)
    o_ref[...] = _dot(s, w_ref[...].astype(BF16)) + b_ref[...]


def modulation(c, w, b):
    bc, d = c.shape
    n = w.shape[1]
    tn = 1024
    return pl.pallas_call(
        _mod_kernel,
        out_shape=jax.ShapeDtypeStruct((bc, n), F32),
        grid=(n // tn,),
        in_specs=[pl.BlockSpec((bc, d), lambda j: (0, 0)),
                  pl.BlockSpec((d, tn), lambda j: (0, j)),
                  pl.BlockSpec((1, tn), lambda j: (0, j))],
        out_specs=pl.BlockSpec((bc, tn), lambda j: (0, j)),
        compiler_params=_cparams(("arbitrary",)),
        name="modulation",
    )(c, w, b.reshape(1, n))


def _rope_tables(pos, lane0, half):
    inv = ROPE_THETA ** (-jnp.arange(half, dtype=F32) / half)
    ang = pos.astype(F32)[:, None] * inv
    cos, sin = jnp.cos(ang), jnp.sin(ang)
    t = pos.shape[0]
    cos_t = jnp.ones((t, LANES), F32)
    sin_t = jnp.zeros((t, LANES), F32)
    cos_t = cos_t.at[:, lane0:lane0 + half].set(cos).at[:, lane0 + half:lane0 + 2 * half].set(cos)
    sin_t = sin_t.at[:, lane0:lane0 + half].set(-sin).at[:, lane0 + half:lane0 + 2 * half].set(sin)
    return cos_t, sin_t


def _rope_slab(s, cos_t, sin_t, lane, lane0, half):
    swapped = jnp.where(lane < lane0 + half, pltpu.roll(s, LANES - half, 1), pltpu.roll(s, half, 1))
    return s * cos_t + swapped * sin_t


MLA_SLAB_ROPE0 = MLA_NOPE


def _mla_proj_kernel(x_ref, sh_ref, sc_ref, cos_ref, sin_ref, wdq_ref, gcq_ref, wuq_ref, wdkv_ref, gckv_ref,
                     gq_ref, gkr_ref, wuk_ref, gkn_ref, wuv_ref,
                     q_ref, k_ref, v_ref, ckv_ref, kpe_ref, *, qscale):
    x = x_ref[...]
    tm = x.shape[0]
    h = _rms(x) * (1.0 + sc_ref[...]) + sh_ref[...]
    hb = h.astype(BF16)
    cq = _rms(_dot(hb, wdq_ref[...])) * gcq_ref[...]
    q = _dot(cq.astype(BF16), wuq_ref[...])
    kv = _dot(hb, wdkv_ref[...])
    ckv = _rms(kv[:, :MLA_KV_LORA]) * gckv_ref[...]
    cos_t, sin_t = cos_ref[...], sin_ref[...]
    lane = lax.broadcasted_iota(I32, (tm, LANES), 1)
    m_n = lane < MLA_NOPE
    m_r = (lane >= MLA_NOPE) & (lane < MLA_NOPE + MLA_ROPE)

    def norm_rope(s, g):
        s2 = s * s
        ssn = jnp.sum(jnp.where(m_n, s2, 0.0), axis=-1, keepdims=True)
        ssr = jnp.sum(jnp.where(m_r, s2, 0.0), axis=-1, keepdims=True)
        r = jnp.where(m_n, lax.rsqrt(ssn / MLA_NOPE + EPS), lax.rsqrt(ssr / MLA_ROPE + EPS))
        return _rope_slab(s * r * g, cos_t, sin_t, lane, MLA_SLAB_ROPE0, MLA_ROPE // 2)

    kpe = norm_rope(kv[:, MLA_KV_LORA:], gkr_ref[...])
    ckv_ref[...] = ckv
    kpe_ref[...] = kpe
    ckvb = ckv.astype(BF16)
    kn = _dot(ckvb, wuk_ref[...])
    v_ref[...] = _dot(ckvb, wuv_ref[...]).astype(BF16)
    gq = gq_ref[...]
    gkn = gkn_ref[...]
    for hh in range(MLA_HEADS):
        sl = slice(hh * LANES, (hh + 1) * LANES)
        q_ref[hh] = (norm_rope(q[:, sl], gq) * qscale).astype(BF16)
        ks = kn[:, sl]
        ssn = jnp.sum(ks * ks, axis=-1, keepdims=True)
        k_ref[hh] = (ks * lax.rsqrt(ssn / MLA_NOPE + EPS) * gkn + kpe).astype(BF16)


def _mla_weights(w_dq, g_cq, w_uq, w_dkv, g_ckv, g_qn, g_qr, g_kr, w_uk, g_kn, w_uv):
    d = w_dq.shape[0]
    hd = MLA_NOPE + MLA_ROPE
    pad = LANES - hd
    wuq = jnp.pad(w_uq.reshape(-1, MLA_HEADS, hd), ((0, 0), (0, 0), (0, pad))).reshape(-1, MLA_HEADS * LANES)
    wdkv = jnp.concatenate([w_dkv[:, :MLA_KV_LORA], jnp.zeros((d, MLA_NOPE), F32), w_dkv[:, MLA_KV_LORA:],
                            jnp.zeros((d, pad), F32)], axis=1)
    gq = jnp.concatenate([g_qn, g_qr, jnp.zeros((pad,), F32)])[None]
    gkr = jnp.concatenate([jnp.zeros((MLA_NOPE,), F32), g_kr, jnp.zeros((pad,), F32)])[None]
    wuk = jnp.pad(w_uk, ((0, 0), (0, 0), (0, LANES - MLA_NOPE))).reshape(MLA_KV_LORA, MLA_HEADS * LANES)
    gkn = jnp.concatenate([g_kn, jnp.zeros((LANES - MLA_NOPE,), F32)])[None]
    wuv = w_uv.reshape(MLA_KV_LORA, MLA_HEADS * MLA_V)
    return dict(wdq=w_dq.astype(BF16), gcq=g_cq[None], wuq=wuq.astype(BF16), wdkv=wdkv.astype(BF16),
                gckv=g_ckv[None], gq=gq, gkr=gkr, wuk=wuk.astype(BF16), gkn=gkn, wuv=wuv.astype(BF16))


def mla_project(x, shift, scale, cos_t, sin_t, w, *, per_row, rows_per_batch, tm):
    n, d = x.shape
    nblk_pos = cos_t.shape[0] // tm
    full = lambda a: pl.BlockSpec(a.shape, lambda i: (0,) * a.ndim)
    wnames = ("wdq", "gcq", "wuq", "wdkv", "gckv", "gq", "gkr", "wuk", "gkn", "wuv")
    qscale = float((MLA_NOPE + MLA_ROPE) ** -0.5)
    return pl.pallas_call(
        functools.partial(_mla_proj_kernel, qscale=qscale),
        out_shape=(jax.ShapeDtypeStruct((MLA_HEADS, n, LANES), BF16),
                   jax.ShapeDtypeStruct((MLA_HEADS, n, LANES), BF16),
                   jax.ShapeDtypeStruct((n, MLA_HEADS * MLA_V), BF16),
                   jax.ShapeDtypeStruct((n, MLA_KV_LORA), F32),
                   jax.ShapeDtypeStruct((n, LANES), F32)),
        grid=(n // tm,),
        in_specs=[pl.BlockSpec((tm, d), lambda i: (i, 0)),
                  _mod_specs(per_row, tm, d, rows_per_batch),
                  _mod_specs(per_row, tm, d, rows_per_batch),
                  pl.BlockSpec((tm, LANES), lambda i: (i % nblk_pos, 0)),
                  pl.BlockSpec((tm, LANES), lambda i: (i % nblk_pos, 0))] + [full(w[k]) for k in wnames],
        out_specs=(pl.BlockSpec((MLA_HEADS, tm, LANES), lambda i: (0, i, 0)),
                   pl.BlockSpec((MLA_HEADS, tm, LANES), lambda i: (0, i, 0)),
                   pl.BlockSpec((tm, MLA_HEADS * MLA_V), lambda i: (i, 0)),
                   pl.BlockSpec((tm, MLA_KV_LORA), lambda i: (i, 0)),
                   pl.BlockSpec((tm, LANES), lambda i: (i, 0))),
        compiler_params=_cparams(("arbitrary",)),
        name="mla_project",
    )(x, shift, scale, cos_t, sin_t, *[w[k] for k in wnames])


def _mla_attn_kernel(q_ref, k_ref, v_ref, o_ref, *, tq):
    qi = pl.program_id(2)
    row = lax.broadcasted_iota(I32, (tq, tq), 0)
    col = lax.broadcasted_iota(I32, (tq, tq), 1)
    nh = q_ref.shape[0]
    hs = range(nh)
    qs = [q_ref[hh] for hh in hs]

    def step(j, carry, masked):
        off = pl.multiple_of(j * tq, tq)
        vcs = [v_ref[pl.ds(off, tq), hp * LANES:(hp + 1) * LANES] for hp in range(nh // 2)]
        ss = [_dot_nt(qs[hh], k_ref[hh, pl.ds(off, tq), :]) for hh in hs]
        if masked:
            ss = [jnp.where(col <= row, s, NEG) for s in ss]
        m_new = [jnp.maximum(carry[hh][0], jnp.max(ss[hh], axis=-1, keepdims=True)) for hh in hs]
        alpha = [jnp.exp(carry[hh][0] - m_new[hh]) for hh in hs]
        ps = [jnp.exp(ss[hh] - m_new[hh]) for hh in hs]
        ls = [alpha[hh] * carry[hh][1] + jnp.sum(ps[hh], axis=-1, keepdims=True) for hh in hs]
        pv = [_dot(ps[hh].astype(BF16), vcs[hh // 2]) for hh in hs]
        return tuple((m_new[hh], ls[hh], alpha[hh] * carry[hh][2] + pv[hh]) for hh in hs)

    init = (jnp.full((tq, 1), NEG, F32), jnp.zeros((tq, 1), F32), jnp.zeros((tq, LANES), F32))
    carry = lax.fori_loop(0, qi, functools.partial(step, masked=False), (init,) * nh)
    carry = step(qi, carry, masked=True)
    lane = lax.broadcasted_iota(I32, (tq, LANES), 1)
    for hp in range(nh // 2):
        even, odd = carry[2 * hp], carry[2 * hp + 1]
        o_ref[:, hp * LANES:(hp + 1) * LANES] = jnp.where(lane < MLA_V, even[2] / even[1], odd[2] / odd[1]).astype(BF16)


MLA_ATTN_HEADS = 4


def mla_attention(q, k, v, b, s):
    tq = 256
    nh = MLA_ATTN_HEADS
    vw = nh * MLA_V
    q4 = q.reshape(MLA_HEADS, b, s, LANES)
    k4 = k.reshape(MLA_HEADS, b, s, LANES)
    v3 = v.reshape(b, s, MLA_HEADS * MLA_V)
    out = pl.pallas_call(
        functools.partial(_mla_attn_kernel, tq=tq),
        out_shape=jax.ShapeDtypeStruct((b, s, MLA_HEADS * MLA_V), BF16),
        grid=(b, MLA_HEADS // nh, s // tq),
        in_specs=[pl.BlockSpec((nh, None, tq, LANES), lambda bi, h, qi: (h, bi, qi, 0)),
                  pl.BlockSpec((nh, None, s, LANES), lambda bi, h, qi: (h, bi, 0, 0)),
                  pl.BlockSpec((None, s, vw), lambda bi, h, qi: (bi, 0, h))],
        out_specs=pl.BlockSpec((None, tq, vw), lambda bi, h, qi: (bi, qi, h)),
        compiler_params=_cparams(("arbitrary", "arbitrary", "arbitrary")),
        name="mla_attention",
    )(q4, k4, v3)
    return out.reshape(b * s, MLA_HEADS * MLA_V)


def _out_proj_kernel(x_ref, o_ref, w_ref, g_ref, y_ref):
    y_ref[...] = x_ref[...] + g_ref[...] * _dot(o_ref[...], w_ref[...])


def out_proj(x, o, w, gate, *, per_row, rows_per_batch, tm):
    n, d = x.shape
    kdim = o.shape[1]
    return pl.pallas_call(
        _out_proj_kernel,
        out_shape=jax.ShapeDtypeStruct((n, d), F32),
        grid=(n // tm,),
        in_specs=[pl.BlockSpec((tm, d), lambda i: (i, 0)),
                  pl.BlockSpec((tm, kdim), lambda i: (i, 0)),
                  pl.BlockSpec((kdim, d), lambda i: (0, 0)),
                  _mod_specs(per_row, tm, d, rows_per_batch)],
        out_specs=pl.BlockSpec((tm, d), lambda i: (i, 0)),
        compiler_params=_cparams(("arbitrary",)),
        name="out_proj",
    )(x, o, w, gate)


MLA_DEC_PAGES = 8
NEW_ROWS = 8


def _qabs_kernel(q_ref, w_ref, o_ref):
    o_ref[...] = _dot(q_ref[...], w_ref[...]).astype(BF16)


def mla_absorb_q(q, w_uk, g_kn):
    h, n, _ = q.shape
    wt = jnp.transpose(w_uk, (1, 2, 0)) * g_kn[None, :, None]
    wt = jnp.pad(wt, ((0, 0), (0, LANES - MLA_NOPE), (0, 0))).astype(BF16)
    return pl.pallas_call(
        _qabs_kernel,
        out_shape=jax.ShapeDtypeStruct((h, n, MLA_KV_LORA), BF16),
        grid=(h,),
        in_specs=[pl.BlockSpec((None, n, LANES), lambda i: (i, 0, 0)),
                  pl.BlockSpec((None, LANES, MLA_KV_LORA), lambda i: (i, 0, 0))],
        out_specs=pl.BlockSpec((None, n, MLA_KV_LORA), lambda i: (i, 0, 0)),
        compiler_params=_cparams(("arbitrary",)),
        name="mla_absorb_q",
    )(q, wt)


def _mla_decode_kernel(pt_ref, qa_ref, qp_ref, new_ref, wukt_ref, *rest, n_dec):
    page_refs = rest[:MLA_DEC_PAGES]
    o_ref = rest[MLA_DEC_PAGES]
    m_ref, l_ref, acc_ref = rest[MLA_DEC_PAGES + 1:]
    j = pl.program_id(1)
    qa = qa_ref[...]
    qp = qp_ref[...]
    nrow = qa.shape[0]

    def scores(lats):
        ckvts = [lat_t[:MLA_KV_LORA].astype(BF16) for lat_t in lats]
        kpets = [lat_t[MLA_KV_LORA:].astype(BF16) for lat_t in lats]
        kns = [_dot(wukt_ref[...], ckvt) for ckvt in ckvts]
        ssqs = [jnp.sum((kn * kn).reshape(MLA_HEADS, MLA_NOPE, kn.shape[1]), axis=1) for kn in kns]
        rs = [lax.rsqrt(ssq / MLA_NOPE + EPS) for ssq in ssqs]
        rrs = [jnp.concatenate([r] * n_dec + [jnp.zeros((nrow - n_dec * MLA_HEADS, r.shape[1]), F32)], axis=0)
               for r in rs]
        return [_dot(qa, ckvt) * rr + _dot(qp, kpet) for ckvt, rr, kpet in zip(ckvts, rrs, kpets)], ckvts

    def accumulate(ss, ckvts):
        m = m_ref[...]
        m_new = m
        for s in ss:
            m_new = jnp.maximum(m_new, jnp.max(s, axis=-1, keepdims=True))
        alpha = jnp.exp(m - m_new)
        ps = [jnp.exp(s - m_new) for s in ss]
        l = alpha * l_ref[...]
        acc = alpha * acc_ref[...]
        for p, ckvt in zip(ps, ckvts):
            l = l + jnp.sum(p, axis=-1, keepdims=True)
            acc = acc + _dot_nt(p.astype(BF16), ckvt)
        l_ref[...] = l
        acc_ref[...] = acc
        m_ref[...] = m_new

    @pl.when(j == 0)
    def _():
        m_ref[...] = jnp.full(m_ref.shape, NEG, F32)
        l_ref[...] = jnp.zeros(l_ref.shape, F32)
        acc_ref[...] = jnp.zeros(acc_ref.shape, F32)
        (s,), ckvts = scores([new_ref[...]])
        key = lax.broadcasted_iota(I32, s.shape, 1)
        tok = lax.broadcasted_iota(I32, s.shape, 0) // MLA_HEADS
        accumulate([jnp.where((key <= tok) & (key < n_dec), s, NEG)], ckvts)

    half = MLA_DEC_PAGES // 2
    accumulate(*scores([jnp.concatenate([r[...] for r in page_refs[:half]], axis=1),
                        jnp.concatenate([r[...] for r in page_refs[half:]], axis=1)]))

    @pl.when(j == pl.num_programs(1) - 1)
    def _():
        o_ref[...] = acc_ref[...] / l_ref[...]


def mla_decode_attention(qabs, qpe, lat_new_t, pool_t, page_table, w_uk, n_dec):
    db, n_pages = page_table.shape
    lat = pool_t.shape[1]
    nrow = qabs.shape[1]
    wukt = w_uk.reshape(MLA_KV_LORA, MLA_HEADS * MLA_NOPE).T.astype(BF16)
    steps = n_pages // MLA_DEC_PAGES
    page_specs = [pl.BlockSpec((None, lat, PAGE_SIZE),
                               lambda b, j, pt, pp=pp: (pt[b * n_pages + j * MLA_DEC_PAGES + pp], 0, 0))
                  for pp in range(MLA_DEC_PAGES)]
    grid_spec = pltpu.PrefetchScalarGridSpec(
        num_scalar_prefetch=1,
        grid=(db, steps),
        in_specs=[pl.BlockSpec((None, nrow, MLA_KV_LORA), lambda b, j, pt: (b, 0, 0)),
                  pl.BlockSpec((None, nrow, MLA_ROPE), lambda b, j, pt: (b, 0, 0)),
                  pl.BlockSpec((None, lat, LANES), lambda b, j, pt: (b, 0, 0)),
                  pl.BlockSpec(wukt.shape, lambda b, j, pt: (0, 0))] + page_specs,
        out_specs=pl.BlockSpec((None, nrow, MLA_KV_LORA), lambda b, j, pt: (b, 0, 0)),
        scratch_shapes=[pltpu.VMEM((nrow, 1), F32), pltpu.VMEM((nrow, 1), F32),
                        pltpu.VMEM((nrow, MLA_KV_LORA), F32)],
    )
    return pl.pallas_call(
        functools.partial(_mla_decode_kernel, n_dec=n_dec),
        out_shape=jax.ShapeDtypeStruct((db, nrow, MLA_KV_LORA), F32),
        grid_spec=grid_spec,
        compiler_params=_cparams(("arbitrary", "arbitrary")),
        name="mla_decode_attention",
    )(page_table.reshape(-1), qabs, qpe, lat_new_t, wukt, *([pool_t] * MLA_DEC_PAGES))


def _mla_uv_kernel(x_ref, w_ref, o_ref):
    w = w_ref[...]
    lane = lax.broadcasted_iota(I32, o_ref.shape, 1)
    o_ref[...] = jnp.where(lane < MLA_V, _dot(x_ref[0], w), _dot(x_ref[1], w)).astype(BF16)


def mla_decode_values(x, w_uv):
    h, n, _ = x.shape
    wuv = w_uv.reshape(MLA_KV_LORA, MLA_HEADS * MLA_V).astype(BF16)
    return pl.pallas_call(
        _mla_uv_kernel,
        out_shape=jax.ShapeDtypeStruct((n, MLA_HEADS * MLA_V), BF16),
        grid=(h // 2,),
        in_specs=[pl.BlockSpec((2, n, MLA_KV_LORA), lambda i: (i, 0, 0)),
                  pl.BlockSpec((MLA_KV_LORA, LANES), lambda i: (0, i))],
        out_specs=pl.BlockSpec((n, LANES), lambda i: (0, i)),
        compiler_params=_cparams(("arbitrary",)),
        name="mla_decode_values",
    )(x, wuv)


def mla_sample_attention(q, ckv, kpe, pool, page_table, w_uk, g_kn, w_uv, n_dec):
    db = page_table.shape[0]
    n = db * n_dec
    qabs = mla_absorb_q(q, w_uk, g_kn)
    nq = MLA_HEADS * n_dec
    assert nq <= LANES

    def to_rows(a):
        f = a.shape[-1]
        a = a.reshape(MLA_HEADS, db, n_dec, f).transpose(1, 2, 0, 3).reshape(db, nq, f)
        return jnp.pad(a, ((0, 0), (0, LANES - nq), (0, 0)))

    lat_new = jnp.concatenate([ckv, kpe[:, MLA_NOPE:MLA_NOPE + MLA_ROPE]], axis=1).reshape(db, n_dec, -1)
    lat_new_t = jnp.pad(jnp.swapaxes(lat_new, 1, 2), ((0, 0), (0, 0), (0, LANES - n_dec)))
    out = mla_decode_attention(to_rows(qabs), to_rows(q[:, :, MLA_NOPE:MLA_NOPE + MLA_ROPE]), lat_new_t,
                               jnp.swapaxes(pool, 1, 2), page_table, w_uk, n_dec)
    x = out[:, :nq].reshape(db, n_dec, MLA_HEADS, MLA_KV_LORA).transpose(2, 0, 1, 3)
    x = x.reshape(MLA_HEADS, n, MLA_KV_LORA).astype(BF16)
    return mla_decode_values(x, w_uv)


def _ada_heads_kernel(x_ref, sh_ref, sc_ref, cos_ref, sin_ref, w_ref, g_ref, o_ref, *, n_norm, n_slab, oscale):
    x = x_ref[...]
    tm = x.shape[0]
    h = _rms(x) * (1.0 + sc_ref[...]) + sh_ref[...]
    y = _dot(h.astype(BF16), w_ref[...])
    cos_t, sin_t = cos_ref[...], sin_ref[...]
    g = g_ref[...]
    lane = lax.broadcasted_iota(I32, (tm, LANES), 1)
    for s in range(n_slab):
        sl = slice(s * LANES, (s + 1) * LANES)
        ys = y[:, sl]
        if s < n_norm:
            ys = _rope_slab(_rms(ys) * g, cos_t, sin_t, lane, 0, DIL_ROT // 2) * oscale
        o_ref[:, sl] = ys


def ada_heads(x, shift, scale, cos_t, sin_t, w, g, *, n_norm, oscale, per_row, rows_per_batch, tm):
    n, d = x.shape
    ng, _, gw = w.shape
    n_slab = gw // LANES
    nblk_pos = cos_t.shape[0] // tm
    mod_spec = (pl.BlockSpec((tm, d), lambda gi, i: (i, 0)) if per_row else
                pl.BlockSpec((None, 1, d), lambda gi, i: ((i * tm) // rows_per_batch, 0, 0)))
    return pl.pallas_call(
        functools.partial(_ada_heads_kernel, n_norm=n_norm, n_slab=n_slab, oscale=oscale),
        out_shape=jax.ShapeDtypeStruct((n, ng * gw), F32),
        grid=(ng, n // tm),
        in_specs=[pl.BlockSpec((tm, d), lambda gi, i: (i, 0)), mod_spec, mod_spec,
                  pl.BlockSpec((tm, LANES), lambda gi, i: (i % nblk_pos, 0)),
                  pl.BlockSpec((tm, LANES), lambda gi, i: (i % nblk_pos, 0)),
                  pl.BlockSpec((None, d, gw), lambda gi, i: (gi, 0, 0)),
                  pl.BlockSpec((None, 1, LANES), lambda gi, i: (gi, 0, 0))],
        out_specs=pl.BlockSpec((tm, gw), lambda gi, i: (i, gi)),
        compiler_params=_cparams(("arbitrary", "arbitrary")),
        name="ada_heads",
    )(x, shift, scale, cos_t, sin_t, w, g)


DIL_SPAN = 128
STAT_DEN0 = DIL_HEADS


def _dil_prompt_kernel(q_ref, kp_ref, vp_ref, kc_ref, vc_ref, num_ref, st_ref, *, dil, hps):
    nb = pl.program_id(1)
    hs = pl.program_id(2)
    span = DIL_SPAN
    iq = lax.broadcasted_iota(I32, (span, span), 0)
    jk = lax.broadcasted_iota(I32, (span, span), 1)
    mask_prev = (jk >= iq) & (nb > 0)
    mask_cur = jk <= iq
    lane = lax.broadcasted_iota(I32, (span, LANES), 1)

    @pl.when(hs == 0)
    def _():
        st_ref[...] = jnp.zeros(st_ref.shape, F32)

    def attend(items):
        n = range(len(items))
        qs = [q_ref[rows, sl].astype(BF16) for rows, sl, _ in items]
        sps = [jnp.where(mask_prev, _dot_nt(qs[i], kp_ref[items[i][0], items[i][1]].astype(BF16)), NEG) for i in n]
        scs = [jnp.where(mask_cur, _dot_nt(qs[i], kc_ref[items[i][0], items[i][1]].astype(BF16)), NEG) for i in n]
        ms = [jnp.maximum(jnp.max(sps[i], axis=-1, keepdims=True), jnp.max(scs[i], axis=-1, keepdims=True)) for i in n]
        pps = [jnp.exp(sps[i] - ms[i]) for i in n]
        pcs = [jnp.exp(scs[i] - ms[i]) for i in n]
        dens = [jnp.sum(pps[i], axis=-1, keepdims=True) + jnp.sum(pcs[i], axis=-1, keepdims=True) for i in n]
        for i in n:
            rows, sl, hh = items[i]
            num_ref[rows, sl] = (_dot(pps[i].astype(BF16), vp_ref[rows, sl].astype(BF16))
                                 + _dot(pcs[i].astype(BF16), vc_ref[rows, sl].astype(BF16)))
        for i in n:
            rows, _, hh = items[i]
            stats = st_ref[rows, :]
            stats = jnp.where(lane == hh, ms[i], stats)
            st_ref[rows, :] = jnp.where(lane == STAT_DEN0 + hh, dens[i], stats)

    if dil > 1:
        per_trip = 2 if dil < 8 else 4

        def residues(u, _):
            attend([(pl.ds(u * per_trip + k, span, stride=dil), slice(0, LANES), hs) for k in range(per_trip)])
            return 0

        lax.fori_loop(0, dil // per_trip, residues, 0)
    else:
        for h0 in range(0, hps, 4):
            attend([(pl.ds(0, span), slice(hl * LANES, (hl + 1) * LANES), hs * hps + hl) for hl in range(h0, h0 + 4)])


DIL_HEADS_PER_STEP = (8, 1, 1)


def dil_prompt_attention(q, kv, g, b, s):
    dil = DIL_RATES[g]
    span = DIL_WINDOWS[g] // dil
    assert span == DIL_SPAN and s % (dil * span) == 0
    rows = span * dil
    n_blk = s // rows
    hps = DIL_HEADS_PER_STEP[g]
    hw = DIL_HEADS * DIL_HEAD_DIM
    bw = hps * DIL_HEAD_DIM
    nh = DIL_HEADS // hps
    q3 = q.reshape(b, s, q.shape[1])
    kv3 = kv.reshape(b, s, kv.shape[1])
    prev = lambda nb: jnp.maximum(nb - 1, 0)
    blk = lambda row_of, col0: pl.BlockSpec((None, rows, bw), lambda bi, nb, hs: (bi, row_of(nb), col0 * nh + hs))
    cur = lambda nb: nb
    num, st = pl.pallas_call(
        functools.partial(_dil_prompt_kernel, dil=dil, hps=hps),
        out_shape=(jax.ShapeDtypeStruct((b, s, hw), F32), jax.ShapeDtypeStruct((b, s, LANES), F32)),
        grid=(b, n_blk, nh),
        in_specs=[blk(cur, g), blk(prev, 2 * g), blk(prev, 2 * g + 1), blk(cur, 2 * g), blk(cur, 2 * g + 1)],
        out_specs=(blk(cur, 0), pl.BlockSpec((None, rows, LANES), lambda bi, nb, hs: (bi, nb, 0))),
        compiler_params=_cparams(("arbitrary", "arbitrary", "arbitrary")),
        name=f"dil_prompt_attention_g{g}",
    )(q3, kv3, kv3, kv3, kv3)
    return num.reshape(b * s, hw), st.reshape(b * s, LANES)


def _dil_sample_kernel(q_ref, cache_ref, new_ref, num_ref, st_ref, *, dil, n_dec):
    t = pl.program_id(1)
    hw = DIL_HEADS * DIL_HEAD_DIM
    rows = cache_ref.shape[0]
    n_new = new_ref.shape[0]
    crow = lax.broadcasted_iota(I32, (rows, 1), 0)
    nrow = lax.broadcasted_iota(I32, (n_new, 1), 0)
    cache_ok = (crow >= t) if dil == 1 else (crow >= 0)
    new_ok = (nrow <= t) & (nrow < n_dec) if dil == 1 else (nrow == t)
    lane = lax.broadcasted_iota(I32, (1, LANES), 1)
    stats = jnp.zeros((1, LANES), F32)
    for hh in range(DIL_HEADS):
        sl = slice(hh * LANES, (hh + 1) * LANES)
        q = q_ref[:, sl]
        sc = jnp.where(cache_ok, jnp.sum(cache_ref[:, 0, hh, :] * q, axis=-1, keepdims=True), NEG)
        sn = jnp.where(new_ok, jnp.sum(new_ref[:, sl] * q, axis=-1, keepdims=True), NEG)
        m = jnp.maximum(jnp.max(sc, axis=0, keepdims=True), jnp.max(sn, axis=0, keepdims=True))
        pc = jnp.exp(sc - m)
        pn = jnp.exp(sn - m)
        den = jnp.sum(pc, axis=0, keepdims=True) + jnp.sum(pn, axis=0, keepdims=True)
        vs = slice(hw + hh * LANES, hw + (hh + 1) * LANES)
        num_ref[:, sl] = (jnp.sum(pc * cache_ref[:, 1, hh, :], axis=0, keepdims=True)
                          + jnp.sum(pn * new_ref[:, vs], axis=0, keepdims=True))
        stats = jnp.where(lane == hh, m, stats)
        stats = jnp.where(lane == STAT_DEN0 + hh, den, stats)
    st_ref[...] = stats


def dil_sample_attention(q, cache, kv_new, g, n_dec):
    dil = DIL_RATES[g]
    db, w = cache.shape[:2]
    assert w == DIL_WINDOWS[g] and w // dil == DIL_SPAN and (dil == 1 or n_dec <= dil)
    hw = DIL_HEADS * DIL_HEAD_DIM
    rows = w // dil
    cv = cache.reshape(db, rows, dil, 2, DIL_HEADS, DIL_HEAD_DIM)
    q4 = q.reshape(db, n_dec, 1, q.shape[1])
    n_new = kv_new.shape[1]
    num, st = pl.pallas_call(
        functools.partial(_dil_sample_kernel, dil=dil, n_dec=n_dec),
        out_shape=(jax.ShapeDtypeStruct((db, n_dec, 1, hw), F32),
                   jax.ShapeDtypeStruct((db, n_dec, 1, LANES), F32)),
        grid=(db, n_dec),
        in_specs=[pl.BlockSpec((None, None, 1, hw), lambda bi, t: (bi, t, 0, g)),
                  pl.BlockSpec((None, rows, None, 2, DIL_HEADS, DIL_HEAD_DIM),
                               lambda bi, t: (bi, 0, t if dil > 1 else 0, 0, 0, 0)),
                  pl.BlockSpec((None, n_new, 2 * hw), lambda bi, t: (bi, 0, g))],
        out_specs=(pl.BlockSpec((None, None, 1, hw), lambda bi, t: (bi, t, 0, 0)),
                   pl.BlockSpec((None, None, 1, LANES), lambda bi, t: (bi, t, 0, 0))),
        compiler_params=_cparams(("arbitrary", "arbitrary")),
        name=f"dil_sample_attention_g{g}",
    )(q4, cv, kv_new)
    return num.reshape(db * n_dec, hw), st.reshape(db * n_dec, LANES)


def _dil_combine_kernel(x_ref, n0_ref, n1_ref, n2_ref, s0_ref, s1_ref, s2_ref, w_ref, g_ref, y_ref):
    nums = (n0_ref, n1_ref, n2_ref)
    stats = (s0_ref[...], s1_ref[...], s2_ref[...])
    outs = []
    for hh in range(DIL_HEADS):
        sl = slice(hh * LANES, (hh + 1) * LANES)
        ms = [st[:, hh:hh + 1] for st in stats]
        ds = [st[:, STAT_DEN0 + hh:STAT_DEN0 + hh + 1] for st in stats]
        big = jnp.maximum(jnp.maximum(ms[0], ms[1]), ms[2])
        num = 0.0
        den = 0.0
        for gi in range(N_GROUPS):
            wgt = jnp.exp(ms[gi] - big)
            num = num + wgt * nums[gi][:, sl]
            den = den + wgt * ds[gi]
        outs.append((num / den).astype(BF16))
    o = jnp.concatenate(outs, axis=1)
    y_ref[...] = x_ref[...] + g_ref[...] * _dot(o, w_ref[...])


def dil_combine(x, nums, stats, w, gate, *, per_row, rows_per_batch, tm):
    n, d = x.shape
    hw = DIL_HEADS * DIL_HEAD_DIM
    row = lambda width: pl.BlockSpec((tm, width), lambda i: (i, 0))
    return pl.pallas_call(
        _dil_combine_kernel,
        out_shape=jax.ShapeDtypeStruct((n, d), F32),
        grid=(n // tm,),
        in_specs=[row(d)] + [row(hw)] * 3 + [row(LANES)] * 3 + [pl.BlockSpec((hw, d), lambda i: (0, 0)),
                                                                 _mod_specs(per_row, tm, d, rows_per_batch)],
        out_specs=row(d),
        compiler_params=_cparams(("arbitrary",)),
        name="dil_combine",
    )(x, *nums, *stats, w, gate)


PEER_CNT = tuple(PEER_TOPK // (a + 1) for a in range(PEER_TOPK))
NEG_INF = float("-inf")


def _peer_route_kernel(x_ref, sh_ref, sc_ref, wq_ref, sk_ref,
                       hb_ref, g_ref, i_ref, j_ref,
                       st_ref, sv_ref, si_ref):
    k = PEER_TOPK
    nk = PEER_NKEYS
    x = x_ref[...]
    tm = x.shape[0]
    nchunk = tm // LANES
    h = _rms(x) * (1.0 + sc_ref[...]) + sh_ref[...]
    hb = h.astype(BF16)
    hb_ref[...] = hb
    q = _dot(hb, wq_ref[...]).astype(BF16)
    st = _dot_nt(sk_ref[...], q)
    for c in range(nchunk):
        st_ref[c] = st[:, c * LANES:(c + 1) * LANES]
    rk = lax.broadcasted_iota(I32, (nk, LANES), 0)
    r16 = lax.broadcasted_iota(I32, (k, LANES), 0)
    r8 = lax.broadcasted_iota(I32, (8, LANES), 0)

    def top_keys(c, hp):
        s = st_ref[c, pl.ds(pl.multiple_of(hp * nk, nk), nk), :]
        sv = jnp.zeros((k, LANES), F32)
        si = jnp.zeros((k, LANES), I32)
        for it in range(k):
            m = jnp.max(s, axis=0, keepdims=True)
            idx = jnp.min(jnp.where(s == m, rk, nk), axis=0, keepdims=True)
            s = jnp.where(rk == idx, NEG_INF, s)
            sv = jnp.where(r16 == it, m, sv)
            si = jnp.where(r16 == it, idx, si)
        sv_ref[c, pl.ds(pl.multiple_of(hp * k, k), k), :] = sv
        si_ref[c, pl.ds(pl.multiple_of(hp * k, k), k), :] = si

    def stage1(t, _):
        c = t // PEER_HEADS
        hh = t % PEER_HEADS
        top_keys(c, 2 * hh)
        top_keys(c, 2 * hh + 1)
        return 0

    lax.fori_loop(0, nchunk * PEER_HEADS, stage1, 0)

    def top_pairs(c, hh):
        o0 = pl.multiple_of(hh * 2 * k, 2 * k)
        o1 = pl.multiple_of(hh * 2 * k + k, k)
        sv0 = sv_ref[c, pl.ds(o0, k), :]
        sv1 = sv_ref[c, pl.ds(o1, k), :]
        si0 = si_ref[c, pl.ds(o0, k), :]
        si1 = si_ref[c, pl.ds(o1, k), :]
        tiles, codes = [], []
        for a in range(8):
            for b0 in range(0, PEER_CNT[a], 8):
                tile = sv0[a:a + 1, :] + sv1[b0:b0 + 8, :]
                tiles.append(jnp.where(r8 + b0 < PEER_CNT[a], tile, NEG_INF))
                codes.append(a * k + b0 + r8)
        tiles.append(sv0[8:16, :] + sv1[0:1, :])
        codes.append((8 + r8) * k)
        pool = jnp.concatenate(tiles, axis=0)
        code = jnp.concatenate(codes, axis=0)
        ts = jnp.zeros((k, LANES), F32)
        ii = jnp.zeros((k, LANES), I32)
        jj = jnp.zeros((k, LANES), I32)
        for it in range(k):
            m = jnp.max(pool, axis=0, keepdims=True)
            sel = jnp.min(jnp.where(pool == m, code, k * k), axis=0, keepdims=True)
            pool = jnp.where(code == sel, NEG_INF, pool)
            iv = jnp.sum(jnp.where(r16 == (sel >> 4), si0, 0), axis=0, keepdims=True)
            jv = jnp.sum(jnp.where(r16 == (sel & (k - 1)), si1, 0), axis=0, keepdims=True)
            ts = jnp.where(r16 == it, m, ts)
            ii = jnp.where(r16 == it, iv, ii)
            jj = jnp.where(r16 == it, jv, jj)
        e = jnp.exp(ts - jnp.max(ts, axis=0, keepdims=True))
        dst = pl.ds(pl.multiple_of(hh * k, k), k)
        g_ref[c, dst, :] = e / jnp.sum(e, axis=0, keepdims=True)
        i_ref[c, dst, :] = ii
        j_ref[c, dst, :] = jj

    def stage2(t, _):
        c = t // (PEER_HEADS // 2)
        hh = (t % (PEER_HEADS // 2)) * 2
        top_pairs(c, hh)
        top_pairs(c, hh + 1)
        return 0

    lax.fori_loop(0, nchunk * (PEER_HEADS // 2), stage2, 0)


def peer_route(x, shift, scale, wq, sk, *, per_row, rows_per_batch, tm):
    n, d = x.shape
    nchunk = tm // LANES
    npick = PEER_HEADS * PEER_TOPK
    nrow = sk.shape[0]
    pick = lambda dt: jax.ShapeDtypeStruct((n // LANES, npick, LANES), dt)
    pick_spec = pl.BlockSpec((nchunk, npick, LANES), lambda i: (i, 0, 0))
    outs = pl.pallas_call(
        _peer_route_kernel,
        out_shape=(jax.ShapeDtypeStruct((n, d), BF16), pick(F32), pick(I32), pick(I32)),
        grid=(n // tm,),
        in_specs=[pl.BlockSpec((tm, d), lambda i: (i, 0)),
                  _mod_specs(per_row, tm, d, rows_per_batch),
                  _mod_specs(per_row, tm, d, rows_per_batch),
                  pl.BlockSpec(wq.shape, lambda i: (0, 0)),
                  pl.BlockSpec(sk.shape, lambda i: (0, 0))],
        out_specs=(pl.BlockSpec((tm, d), lambda i: (i, 0)),) + (pick_spec,) * 3,
        scratch_shapes=[pltpu.VMEM((nchunk, nrow, LANES), F32),
                        pltpu.VMEM((nchunk, 2 * npick, LANES), F32),
                        pltpu.VMEM((nchunk, 2 * npick, LANES), I32)],
        compiler_params=_cparams(("arbitrary",)),
        name="peer_route",
    )(x, shift, scale, wq, sk)
    to_rows = lambda a: jnp.swapaxes(a, 1, 2).reshape(n, npick)
    return (outs[0],) + tuple(to_rows(a) for a in outs[1:])


def _peer_subkey_matrix(subkeys):
    h, p, nk, dk = subkeys.shape
    eye = jnp.eye(h * p, dtype=F32)
    m = eye[:, None, :, None] * subkeys.reshape(h * p, nk, 1, dk)
    return m.reshape(h * p * nk, h * p * dk).astype(BF16)


PEER_EB = 2048
PEER_UP_CHUNK = 256


def _gather_lanes(x, idx):
    return jnp.take_along_axis(x, idx, axis=1, mode="promise_in_bounds")


def _peer_up_kernel(hb_ref, u_ref, i_ref, j_ref, g_ref, c_ref, acc_ref):
    s = pl.program_id(1)
    tm = hb_ref.shape[0]
    nslab = PEER_EB // LANES

    @pl.when(s == 0)
    def _():
        acc_ref[...] = jnp.zeros(acc_ref.shape, F32)

    hb = hb_ref[...]
    ii = i_ref[...]
    jj = j_ref[...]
    acc = acc_ref[...]
    for ch in range(PEER_EB // PEER_UP_CHUNK):
        a = _dot_nt(hb, u_ref[ch * PEER_UP_CHUNK:(ch + 1) * PEER_UP_CHUNK, :])
        for k in range(PEER_UP_CHUNK // LANES):
            sl = ch * (PEER_UP_CHUNK // LANES) + k
            got = _gather_lanes(a[:, k * LANES:(k + 1) * LANES], jj)
            acc = acc + jnp.where(ii == s * nslab + sl, got, 0.0)
    acc_ref[...] = acc

    @pl.when(s == pl.num_programs(1) - 1)
    def _():
        a = acc_ref[...]
        c_ref[...] = g_ref[...] * (0.5 * a * (1.0 + lax.erf(a * float(2.0 ** -0.5))))


def peer_up(hb, u, ii, jj, g, *, tm):
    n, d = hb.shape
    npick = ii.shape[1]
    pick_spec = pl.BlockSpec((tm, npick), lambda i, s: (i, 0))
    return pl.pallas_call(
        _peer_up_kernel,
        out_shape=jax.ShapeDtypeStruct((n, npick), F32),
        grid=(n // tm, u.shape[0] // PEER_EB),
        in_specs=[pl.BlockSpec((tm, d), lambda i, s: (i, 0)),
                  pl.BlockSpec((PEER_EB, d), lambda i, s: (s, 0)),
                  pick_spec, pick_spec, pick_spec],
        out_specs=pick_spec,
        scratch_shapes=[pltpu.VMEM((tm, npick), F32)],
        compiler_params=_cparams(("arbitrary", "arbitrary")),
        name="peer_up",
    )(hb, u, ii, jj, g)


PEER_TG = 16
PEER_TU = 8


def _peer_down_kernel(x_ref, gate_ref, c_ref, i_ref, j_ref, v_ref, y_ref, wa_ref, stage_ref, acc_ref, *, nsteps):
    r = pl.program_id(0)
    s = pl.program_id(1)
    tm = x_ref.shape[0]
    nk = PEER_NKEYS
    nslab = PEER_EB // LANES
    ngroups = tm // PEER_TG
    build = r % 2
    straight_line = ngroups >= nsteps

    @pl.when((r == 0) & (s == 0))
    def _():
        wa_ref[1] = jnp.zeros(wa_ref.shape[1:], BF16)
        acc_ref[...] = jnp.zeros(acc_ref.shape, F32)

    key = lax.broadcasted_iota(I32, (nk, LANES), 0)

    def build_group(gi):
        t0 = pl.multiple_of(gi * PEER_TG, PEER_TG)

        def tokens(u, _):
            for tl in range(PEER_TU):
                tt = u * PEER_TU + tl
                row = pl.ds(t0 + tt, 1)
                ei = jnp.where(key == i_ref[row, :], 1.0, 0.0).astype(BF16)
                cj = jnp.where(key == j_ref[row, :], c_ref[row, :], 0.0).astype(BF16)
                stage_ref[tt] = _dot_nt(ei, cj)
            return 0

        if straight_line:
            for u in range(PEER_TG // PEER_TU):
                tokens(u, 0)
        else:
            lax.fori_loop(0, PEER_TG // PEER_TU, tokens, 0)
        slabs = jnp.swapaxes(stage_ref[...], 0, 1).astype(BF16)
        for i in range(nk):
            wa_ref[build, i // nslab, pl.ds(t0, PEER_TG), (i % nslab) * LANES:(i % nslab + 1) * LANES] = slabs[i]

    if ngroups >= nsteps:
        for k in range(ngroups // nsteps):
            build_group(s * (ngroups // nsteps) + k)
    else:
        pl.when(s < ngroups)(lambda: build_group(s))

    acc = jnp.where(s == 0, 0.0, acc_ref[...]) + _dot(wa_ref[1 - build, s], v_ref[...])
    acc_ref[...] = acc

    @pl.when(s == nsteps - 1)
    def _():
        y_ref[...] = x_ref[...] + gate_ref[...] * acc_ref[...]


def peer_down(x, gate, c, ii, jj, v, *, per_row, rows_per_batch, tm):
    n, d = x.shape
    npick = c.shape[1]
    nsteps = v.shape[0] // PEER_EB
    ntiles = n // tm
    ngroups = tm // PEER_TG
    assert ngroups % nsteps == 0 or ngroups < nsteps
    built = lambda r: jnp.minimum(r, ntiles - 1)
    used = lambda r: jnp.maximum(r - 1, 0)
    pick_spec = pl.BlockSpec((tm, npick), lambda r, s: (built(r), 0))
    gate_spec = (pl.BlockSpec((tm, d), lambda r, s: (used(r), 0)) if per_row else
                 pl.BlockSpec((None, 1, d), lambda r, s: ((used(r) * tm) // rows_per_batch, 0, 0)))
    return pl.pallas_call(
        functools.partial(_peer_down_kernel, nsteps=nsteps),
        out_shape=jax.ShapeDtypeStruct((n, d), F32),
        grid=(ntiles + 1, nsteps),
        in_specs=[pl.BlockSpec((tm, d), lambda r, s: (used(r), 0)), gate_spec,
                  pick_spec, pick_spec, pick_spec,
                  pl.BlockSpec((PEER_EB, d), lambda r, s: (s, 0))],
        out_specs=pl.BlockSpec((tm, d), lambda r, s: (used(r), 0)),
        scratch_shapes=[pltpu.VMEM((2, nsteps, tm, PEER_EB), BF16),
                        pltpu.VMEM((PEER_TG, PEER_NKEYS, LANES), F32),
                        pltpu.VMEM((tm, d), F32)],
        compiler_params=_cparams(("arbitrary", "arbitrary")),
        name="peer_down",
    )(x, gate, c, ii, jj, v)


def peer_sublayer(x, shift, scale, gate, wq, sk, u, v, *, per_row, rows_per_batch, tm_up, tm_down):
    hb, g, ii, jj = peer_route(x, shift, scale, wq, sk, per_row=per_row, rows_per_batch=rows_per_batch, tm=tm_up)
    c = peer_up(hb, u, ii, jj, g, tm=tm_up)
    return peer_down(x, gate, c, ii, jj, v, per_row=per_row, rows_per_batch=rows_per_batch, tm=tm_down)


TM_PROJ = 256
TM_PEER = 512
TM_PEER_DOWN = 256


def kernel(x_prompt, x_sample, c_prompt, c_sample, cache_mla, cache_dil0, cache_dil1, cache_dil2, page_table, a_mod_w, a_mod_b, a_w_dq, a_g_cq, a_w_uq, a_w_dkv, a_g_ckv, a_g_qn, a_g_qr, a_g_kr, a_w_uk, a_g_kn, a_w_uv, a_w_o, kv_mod_w, kv_mod_b, kv_w, kv_g_k, b_mod_w, b_mod_b, b_w_q, b_g_q, b_w_o, f_mod_w, f_mod_b, f_w_q, f_subkeys, f_u, f_v):
    b, s, d = x_prompt.shape
    db, t, _ = x_sample.shape
    n_p, n_s = b * s, db * t
    past = page_table.shape[1] * PAGE_SIZE
    depth = f_mod_w.shape[0]
    n_a = a_mod_w.shape[0]
    caches = (cache_dil0, cache_dil1, cache_dil2)
    hw = DIL_HEADS * DIL_HEAD_DIM
    lat_w = cache_mla.shape[-1]

    c_all = jnp.concatenate([c_prompt, c_sample], axis=0)
    xp = x_prompt.reshape(n_p, d)
    xs = x_sample.reshape(n_s, d)
    pos_p = jnp.arange(s, dtype=I32)
    pos_s = jnp.tile(past + jnp.arange(t, dtype=I32), db)
    mla_tab_p = _rope_tables(pos_p, MLA_SLAB_ROPE0, MLA_ROPE // 2)
    mla_tab_s = _rope_tables(pos_s, MLA_SLAB_ROPE0, MLA_ROPE // 2)
    dil_tab_p = _rope_tables(pos_p, 0, DIL_ROT // 2)
    dil_tab_s = _rope_tables(pos_s, 0, DIL_ROT // 2)
    kw_p = dict(per_row=False, rows_per_batch=s)
    kw_s = dict(per_row=True, rows_per_batch=t)

    def mods(w, bias):
        m = modulation(c_all, w, bias)
        ms = jnp.repeat(m[b:], t, axis=0)
        k = w.shape[1] // d
        return ([m[:b, None, i * d:(i + 1) * d] for i in range(k)], [ms[:, i * d:(i + 1) * d] for i in range(k)])

    rows_p, rows_s = [], []
    dil_p, dil_s = [], []
    kvf_p = kv_new = None
    for layer in range(depth):
        if layer < n_a:
            i = layer
            (sh_p, sc_p, gt_p), (sh_s, sc_s, gt_s) = mods(a_mod_w[i], a_mod_b[i])
            w = _mla_weights(a_w_dq[i], a_g_cq[i], a_w_uq[i], a_w_dkv[i], a_g_ckv[i], a_g_qn[i], a_g_qr[i],
                             a_g_kr[i], a_w_uk[i], a_g_kn[i], a_w_uv[i])
            wo = a_w_o[i].astype(BF16)
            q, k, v, ckv, kpe = mla_project(xp, sh_p, sc_p, *mla_tab_p, w, tm=TM_PROJ, **kw_p)
            rows_p.append(jnp.concatenate([ckv, kpe[:, MLA_NOPE:MLA_NOPE + MLA_ROPE]], axis=1).reshape(b, s, lat_w))
            o = mla_attention(q, k, v, b, s)
            xp = out_proj(xp, o, wo, gt_p, tm=TM_PROJ, **kw_p)
            q, k, v, ckv, kpe = mla_project(xs, sh_s, sc_s, *mla_tab_s, w, tm=n_s, **kw_s)
            rows_s.append(jnp.concatenate([ckv, kpe[:, MLA_NOPE:MLA_NOPE + MLA_ROPE]], axis=1).reshape(db, t, lat_w))
            o = mla_sample_attention(q, ckv, kpe, cache_mla[i], page_table, a_w_uk[i], a_g_kn[i], a_w_uv[i], t)
            xs = out_proj(xs, o, wo, gt_s, tm=n_s, **kw_s)
        else:
            if layer == n_a:
                (sh_p, sc_p), (sh_s, sc_s) = mods(kv_mod_w, kv_mod_b)
                kvw = kv_w.reshape(d, 2, N_GROUPS, hw)
                wkv = jnp.stack([jnp.concatenate([kvw[:, 0, g], kvw[:, 1, g]], axis=1) for g in range(N_GROUPS)])
                wkv = wkv.astype(BF16)
                gk = kv_g_k[:, None, :]
                kvf_p = ada_heads(xp, sh_p, sc_p, *dil_tab_p, wkv, gk, n_norm=DIL_HEADS, oscale=1.0,
                                  tm=TM_PROJ, **kw_p)
                kvf_s = ada_heads(xs, sh_s, sc_s, *dil_tab_s, wkv, gk, n_norm=DIL_HEADS, oscale=1.0,
                                  tm=n_s, **kw_s)
                new_p = kvf_p.reshape(b, s, N_GROUPS, 2, DIL_HEADS, DIL_HEAD_DIM)
                new_s = kvf_s.reshape(db, t, N_GROUPS, 2, DIL_HEADS, DIL_HEAD_DIM)
                for g in range(N_GROUPS):
                    win = DIL_WINDOWS[g]
                    full_s = jnp.concatenate([caches[g], new_s[:, :, g]], axis=1)
                    dil_p.append(new_p[:, s - min(win, s):, g])
                    dil_s.append(full_s[:, full_s.shape[1] - min(win, past + t):])
                kv_new = jnp.pad(kvf_s.reshape(db, t, -1), ((0, 0), (0, NEW_ROWS - t), (0, 0)))
            j = layer - n_a
            (sh_p, sc_p, gt_p), (sh_s, sc_s, gt_s) = mods(b_mod_w[j], b_mod_b[j])
            wq = jnp.transpose(b_w_q[j].reshape(d, N_GROUPS, hw), (1, 0, 2)).astype(BF16)
            gq = b_g_q[j][:, None, :]
            wo = b_w_o[j].astype(BF16)
            qscale = float(DIL_HEAD_DIM ** -0.5)
            qf = ada_heads(xp, sh_p, sc_p, *dil_tab_p, wq, gq, n_norm=DIL_HEADS, oscale=qscale, tm=TM_PROJ, **kw_p)
            parts = [dil_prompt_attention(qf, kvf_p, g, b, s) for g in range(N_GROUPS)]
            xp = dil_combine(xp, [p[0] for p in parts], [p[1] for p in parts], wo, gt_p, tm=TM_PROJ, **kw_p)
            qf = ada_heads(xs, sh_s, sc_s, *dil_tab_s, wq, gq, n_norm=DIL_HEADS, oscale=qscale, tm=n_s, **kw_s)
            parts = [dil_sample_attention(qf, caches[g], kv_new, g, t) for g in range(N_GROUPS)]
            xs = dil_combine(xs, [p[0] for p in parts], [p[1] for p in parts], wo, gt_s, tm=n_s, **kw_s)
        (sh_p, sc_p, gt_p), (sh_s, sc_s, gt_s) = mods(f_mod_w[layer], f_mod_b[layer])
        wq = f_w_q[layer].astype(BF16)
        sk = _peer_subkey_matrix(f_subkeys[layer])
        u = f_u[layer].astype(BF16)
        v = f_v[layer].astype(BF16)
        xp = peer_sublayer(xp, sh_p, sc_p, gt_p, wq, sk, u, v, tm_up=TM_PEER, tm_down=TM_PEER_DOWN, **kw_p)
        xs = peer_sublayer(xs, sh_s, sc_s, gt_s, wq, sk, u, v, tm_up=n_s, tm_down=n_s, **kw_s)
    return (xp.reshape(b, s, d), xs.reshape(db, t, d), jnp.stack(rows_p), jnp.stack(rows_s),
            dil_p[0], dil_s[0], dil_p[1], dil_s[1], dil_p[2], dil_s[2])
```
